```python
import math
import jax, jax.numpy as jnp
from jax import lax
import numpy as np


D_MODEL = 1024
BATCH = 4
SEQ = 8192
DEPTH = 1

HEAD_DIM = 64
SB_HEADS = 8
SB_WIDTH = SB_HEADS * HEAD_DIM
RW_HEADS = 8
RW_WIDTH = RW_HEADS * HEAD_DIM
DECAY_LORA = 64
ICLR_LORA = 64
GATE_LORA = 160
N_BRANCHES = 2
SHIFT_WIDTH = 3 * RW_WIDTH + DECAY_LORA + ICLR_LORA + GATE_LORA
IN_COLS = 3 * SB_WIDTH + SHIFT_WIDTH + N_BRANCHES * D_MODEL
N_GROUPS = 4
EXPERTS_PER_GROUP = 8
N_EXPERTS = N_GROUPS * EXPERTS_PER_GROUP
TOP_K_IN_GROUP = 2
D_EXPERT = 512
Q_BLOCK = 128
NORM_EPS = 1e-6
GN_EPS = 64e-5
L2_EPS = 1e-12

kernel_name = 'hybrid_stickbreak_rwkv7_hmoe_block'


def rmsnorm(x, g):
    xf = x.astype(jnp.float32)
    inv = lax.rsqrt(jnp.mean(xf * xf, axis=-1, keepdims=True) + NORM_EPS)
    return (xf * inv).astype(x.dtype) * g


def stick_breaking_attention(q, k, v):
    seq_len = q.shape[2]
    scale = 1.0 / math.sqrt(q.shape[-1])
    outs = []
    for i in range(seq_len // Q_BLOCK):
        q0 = i * Q_BLOCK
        n_keys = q0 + Q_BLOCK
        qb = q[:, :, q0:n_keys].astype(jnp.float32)
        kb = k[:, :, :n_keys].astype(jnp.float32)
        vb = v[:, :, :n_keys].astype(jnp.float32)
        z = jnp.einsum('bhqd,bhkd->bhqk', qb, kb) * scale
        t_idx = q0 + jnp.arange(Q_BLOCK)[:, None]
        s_idx = jnp.arange(n_keys)[None, :]
        strict = s_idx < t_idx
        log_keep = jnp.where(strict, jax.nn.log_sigmoid(-z), 0.0)
        later = lax.cumsum(log_keep, axis=3, reverse=True) - log_keep
        att = jnp.where(strict, jnp.exp(jax.nn.log_sigmoid(z) + later), 0.0)
        outs.append(jnp.einsum('bhqk,bhkd->bhqd', att, vb).astype(v.dtype))
    return jnp.concatenate(outs, axis=2)


def token_shift(p, mu):
    prev = jnp.pad(p[:, :-1], ((0, 0), (1, 0), (0, 0)))
    return p + mu * (prev - p)


def rwkv7_time_mix(cols, shift_mu, w0, w_decay_up, a0, w_iclr_up, w_gate_up, k_k, k_a, r_k, lnx_w, lnx_b):
    b, s, _ = cols.shape
    xs = token_shift(cols, shift_mu)
    r, k, v, wd, ad, gd = jnp.split(
        xs, [RW_WIDTH, 2 * RW_WIDTH, 3 * RW_WIDTH, 3 * RW_WIDTH + DECAY_LORA,
             3 * RW_WIDTH + DECAY_LORA + ICLR_LORA], axis=-1)
    w = -jax.nn.softplus(-(w0 + jnp.tanh(wd) @ w_decay_up)) - 0.5
    decay = jnp.exp(-jnp.exp(w.astype(jnp.float32)))
    a = jax.nn.sigmoid(a0 + ad @ w_iclr_up)
    g = jax.nn.sigmoid(gd) @ w_gate_up
    kk = k * k_k
    k = k * (1.0 + (a - 1.0) * k_a)

    def heads(t):
        return t.astype(jnp.float32).reshape(b, s, RW_HEADS, HEAD_DIM)

    r_h, k_h, v_h, a_h, kk_h, dec_h = heads(r), heads(k), heads(v), heads(a), heads(kk), heads(decay)
    kk_h = kk_h / jnp.maximum(jnp.sqrt(jnp.sum(kk_h * kk_h, axis=-1, keepdims=True)), L2_EPS)
    vec_a = -kk_h
    vec_b = kk_h * a_h

    def step(state, inp):
        r_t, dec_t, k_t, v_t, a_t, b_t = inp
        sa = jnp.einsum('bhvk,bhk->bhv', state, a_t)
        state = (state * dec_t[:, :, None, :] + sa[..., None] * b_t[:, :, None, :]
                 + v_t[..., None] * k_t[:, :, None, :])
        return state, jnp.einsum('bhvk,bhk->bhv', state, r_t)

    tm = lambda t: jnp.moveaxis(t, 1, 0)
    state0 = jnp.zeros((b, RW_HEADS, HEAD_DIM, HEAD_DIM), jnp.float32)
    _, y = lax.scan(step, state0, (tm(r_h), tm(dec_h), tm(k_h), tm(v_h), tm(vec_a), tm(vec_b)))
    y = jnp.moveaxis(y, 0, 1)
    mean = jnp.mean(y, axis=-1, keepdims=True)
    var = jnp.mean(jnp.square(y - mean), axis=-1, keepdims=True)
    yn = ((y - mean) * lax.rsqrt(var + GN_EPS)).reshape(b, s, RW_WIDTH)
    yn = yn * lnx_w.astype(jnp.float32) + lnx_b.astype(jnp.float32)
    bonus = jnp.sum(r_h * k_h * r_k.astype(jnp.float32), axis=-1, keepdims=True) * v_h
    out = (yn + bonus.reshape(b, s, RW_WIDTH)) * g.astype(jnp.float32)
    return out.astype(cols.dtype)


def hybrid_mixer(h, w_in, gate_bias, shift_mu, w0, w_decay_up, a0, w_iclr_up, w_gate_up,
                 k_k, k_a, r_k, lnx_w, lnx_b, w_o_sb, w_o_rw, w_out):
    b, s, _ = h.shape
    p = h @ w_in
    q, k, v, rw_cols, gate_logits = jnp.split(
        p, [SB_WIDTH, 2 * SB_WIDTH, 3 * SB_WIDTH, 3 * SB_WIDTH + SHIFT_WIDTH], axis=-1)
    to_bhsd = lambda t: t.reshape(b, s, SB_HEADS, HEAD_DIM).transpose(0, 2, 1, 3)
    y_sb = stick_breaking_attention(to_bhsd(q), to_bhsd(k), to_bhsd(v))
    y_sb = y_sb.transpose(0, 2, 1, 3).reshape(b, s, SB_WIDTH)
    y_rw = rwkv7_time_mix(rw_cols, shift_mu, w0, w_decay_up, a0, w_iclr_up, w_gate_up,
                          k_k, k_a, r_k, lnx_w, lnx_b)
    gates = jax.nn.sigmoid(gate_logits + gate_bias)
    g_sb, g_rw = jnp.split(gates, 2, axis=-1)
    merged = g_sb * (y_sb @ w_o_sb) + g_rw * (y_rw @ w_o_rw)
    return merged @ w_out


def hierarchical_moe(h, w_group, w_router, w_up1, w_up3, w_down):
    b, s, d = h.shape
    t = h.reshape(b * s, d)
    n = t.shape[0]
    group_logits = (t @ w_group).astype(jnp.float32)
    group_prob = jax.nn.softmax(group_logits, axis=-1)
    group_sel = jnp.argmax(group_logits, axis=-1)
    group_w = jnp.max(group_prob, axis=-1, keepdims=True)
    expert_logits = (t @ w_router).astype(jnp.float32).reshape(n, N_GROUPS, EXPERTS_PER_GROUP)
    in_group = jnp.einsum('nge,ng->ne', expert_logits, jax.nn.one_hot(group_sel, N_GROUPS, dtype=jnp.float32))
    top_p, top_i = lax.top_k(jax.nn.softmax(in_group, axis=-1), TOP_K_IN_GROUP)
    top_p = top_p / jnp.sum(top_p, axis=-1, keepdims=True)
    expert_ids = group_sel[:, None] * EXPERTS_PER_GROUP + top_i
    combine = jnp.sum(jax.nn.one_hot(expert_ids, N_EXPERTS, dtype=jnp.float32)
                      * (group_w * top_p)[..., None], axis=1).astype(t.dtype)
    y = jnp.zeros_like(t)
    for e in range(N_EXPERTS):
        hid = jax.nn.silu(t @ w_up1[e]) * (t @ w_up3[e])
        y = y + combine[:, e:e + 1] * (hid @ w_down[e])
    return y.reshape(b, s, d)


def setup_inputs(seed: int = 0) -> dict:
    key = jax.random.key(seed)
    ks = jax.random.split(key, 32)
    f32 = jnp.float32
    L = DEPTH

    def nrm(k, shape, scale):
        return jax.random.normal(k, shape, f32) * scale

    return {
        'x': nrm(ks[0], (BATCH, SEQ, D_MODEL), 1.0),
        'c': nrm(ks[1], (BATCH, D_MODEL), 1.0),
        'w_mod': nrm(ks[2], (L, D_MODEL, 6 * D_MODEL), 0.5 * D_MODEL ** -0.5),
        'b_mod': nrm(ks[3], (L, 6 * D_MODEL), 0.01),
        'norm_gains': 1.0 + nrm(ks[4], (L, 4, D_MODEL), 0.02),
        'w_in': nrm(ks[5], (L, D_MODEL, IN_COLS), D_MODEL ** -0.5),
        'gate_bias': nrm(ks[6], (L, N_BRANCHES * D_MODEL), 0.01),
        'shift_mu': jax.random.uniform(ks[7], (L, SHIFT_WIDTH), f32),
        'w0': nrm(ks[8], (L, RW_WIDTH), 0.5),
        'w_decay_up': nrm(ks[9], (L, DECAY_LORA, RW_WIDTH), 0.5 * DECAY_LORA ** -0.5),
        'a0': nrm(ks[10], (L, RW_WIDTH), 0.1),
        'w_iclr_up': nrm(ks[11], (L, ICLR_LORA, RW_WIDTH), ICLR_LORA ** -0.5),
        'w_gate_up': nrm(ks[12], (L, GATE_LORA, RW_WIDTH), GATE_LORA ** -0.5),
        'k_k': 0.85 + nrm(ks[13], (L, RW_WIDTH), 0.05),
        'k_a': 1.0 + nrm(ks[14], (L, RW_WIDTH), 0.05),
        'r_k': nrm(ks[15], (L, RW_HEADS, HEAD_DIM), 0.1),
        'lnx_w': 1.0 + nrm(ks[16], (L, RW_WIDTH), 0.02),
        'lnx_b': nrm(ks[17], (L, RW_WIDTH), 0.01),
        'w_o_sb': nrm(ks[18], (L, SB_WIDTH, D_MODEL), SB_WIDTH ** -0.5),
        'w_o_rw': nrm(ks[19], (L, RW_WIDTH, D_MODEL), RW_WIDTH ** -0.5),
        'w_out': nrm(ks[20], (L, D_MODEL, D_MODEL), D_MODEL ** -0.5),
        'w_group': nrm(ks[21], (L, D_MODEL, N_GROUPS), D_MODEL ** -0.5),
        'w_router': nrm(ks[22], (L, D_MODEL, N_EXPERTS), D_MODEL ** -0.5),
        'w_up1': nrm(ks[23], (L, N_EXPERTS, D_MODEL, D_EXPERT), D_MODEL ** -0.5),
        'w_up3': nrm(ks[24], (L, N_EXPERTS, D_MODEL, D_EXPERT), D_MODEL ** -0.5),
        'w_down': nrm(ks[25], (L, N_EXPERTS, D_EXPERT, D_MODEL), D_EXPERT ** -0.5),
    }


def reference(x, c, w_mod, b_mod, norm_gains, w_in, gate_bias, shift_mu, w0, w_decay_up, a0,
              w_iclr_up, w_gate_up, k_k, k_a, r_k, lnx_w, lnx_b, w_o_sb, w_o_rw, w_out,
              w_group, w_router, w_up1, w_up3, w_down):
    for l in range(DEPTH):
        mod = jax.nn.silu(c) @ w_mod[l] + b_mod[l]
        sh1, sc1, gt1, sh2, sc2, gt2 = [m[:, None, :] for m in jnp.split(mod, 6, axis=-1)]
        h = rmsnorm(x, norm_gains[l, 0]) * (1.0 + sc1) + sh1
        y = hybrid_mixer(h, w_in[l], gate_bias[l], shift_mu[l], w0[l], w_decay_up[l], a0[l],
                         w_iclr_up[l], w_gate_up[l], k_k[l], k_a[l], r_k[l], lnx_w[l], lnx_b[l],
                         w_o_sb[l], w_o_rw[l], w_out[l])
        x = x + gt1 * rmsnorm(y, norm_gains[l, 1])
        h = rmsnorm(x, norm_gains[l, 2]) * (1.0 + sc2) + sh2
        y = hierarchical_moe(h, w_group[l], w_router[l], w_up1[l], w_up3[l], w_down[l])
        x = x + gt2 * rmsnorm(y, norm_gains[l, 3])
    return x
```

```python
import functools
import math

import jax
import jax.numpy as jnp
from jax import lax
from jax.experimental import pallas as pl
from jax.experimental.pallas import tpu as pltpu

F32 = jnp.float32
BF16 = jnp.bfloat16
HIGHEST = lax.Precision.HIGHEST

HEAD_DIM = 64
TOP_K_IN_GROUP = 2
NORM_EPS = 1e-6
GN_EPS = 64e-5
L2_EPS = 1e-12

LANES = 128
SUBLANES = 8
VMEM_LIMIT_BYTES = 56 * 1024 * 1024

TM_PROJ = 512
T_ATT = 256
TM_PREP = 256
C_SCAN = 64
TM_MERGE = 256
TM_EXP = 256
TM_COMB = 256


def _params(*sem):
    return pltpu.CompilerParams(dimension_semantics=sem, vmem_limit_bytes=VMEM_LIMIT_BYTES)


def _dot(a, b):
    return jnp.dot(a.astype(BF16), b.astype(BF16), preferred_element_type=F32)


def _dot_nt(a, b):
    return lax.dot_general(a.astype(BF16), b.astype(BF16), (((1,), (1,)), ((), ())),
                           preferred_element_type=F32)


def _hdot(a, b):
    return jnp.dot(a, b, precision=HIGHEST, preferred_element_type=F32)


def _hdot_nt(a, b):
    return lax.dot_general(a, b, (((1,), (1,)), ((), ())), precision=HIGHEST,
                           preferred_element_type=F32)


def _hdot_tn(a, b):
    return lax.dot_general(a, b, (((0,), (0,)), ((), ())), precision=HIGHEST,
                           preferred_element_type=F32)


def _rms(x):
    return x * lax.rsqrt(jnp.mean(x * x, axis=-1, keepdims=True) + NORM_EPS)


def _sigmoid(x):
    return 1.0 / (1.0 + jnp.exp(-x))


def _mod_kernel(c_ref, w_ref, b_ref, o_ref):
    c = c_ref[...]
    o_ref[...] = _hdot(c * _sigmoid(c), w_ref[...]) + b_ref[...]


def _modulation(c, w_mod, b_mod):
    b, d = c.shape
    n_out = w_mod.shape[1]
    rows = -(-b // SUBLANES) * SUBLANES
    c_pad = jnp.zeros((rows, d), F32).at[:b].set(c)
    out = pl.pallas_call(
        _mod_kernel,
        grid=(n_out // d,),
        in_specs=[pl.BlockSpec((rows, d), lambda j: (0, 0)),
                  pl.BlockSpec((d, d), lambda j: (0, j)),
                  pl.BlockSpec((1, d), lambda j: (0, j))],
        out_specs=pl.BlockSpec((rows, d), lambda j: (0, j)),
        out_shape=jax.ShapeDtypeStruct((rows, n_out), F32),
        compiler_params=_params("parallel"),
        name="mod",
    )(c_pad, w_mod, b_mod.reshape(1, n_out))
    return out[:b]


def _inproj_kernel(x_ref, g_ref, sc_ref, sh_ref, wq_ref, wr_ref, wl_ref, wg_ref,
                   qkv_ref, rkv_ref, lora_ref, gate_ref):
    h = _rms(x_ref[...]) * g_ref[...] * (1.0 + sc_ref[...]) + sh_ref[...]
    hb = h.astype(BF16)
    qkv_ref[...] = jnp.dot(hb, wq_ref[...], preferred_element_type=F32).astype(qkv_ref.dtype)
    rkv_ref[...] = jnp.dot(hb, wr_ref[...], preferred_element_type=F32)
    lora_ref[...] = jnp.dot(hb, wl_ref[...], preferred_element_type=F32)
    gate_ref[...] = jnp.dot(hb, wg_ref[...], preferred_element_type=F32)


def _const_spec(shape):
    return pl.BlockSpec(shape, lambda *_: (0,) * len(shape))


def _batch_spec(d, tiles_per_batch):
    return pl.BlockSpec((None, 1, d), lambda i: (i // tiles_per_batch, 0, 0))


def _inproj(x2, gain, sc, sh, wq, wr, wl, wg, seq):
    n, d = x2.shape
    tm = min(TM_PROJ, seq)
    tpb = seq // tm
    outs = [(wq.shape[1], BF16), (wr.shape[1], F32), (wl.shape[1], F32), (wg.shape[1], F32)]
    return pl.pallas_call(
        _inproj_kernel,
        grid=(n // tm,),
        in_specs=[pl.BlockSpec((tm, d), lambda i: (i, 0)),
                  _const_spec((1, d)), _batch_spec(d, tpb), _batch_spec(d, tpb),
                  _const_spec(wq.shape), _const_spec(wr.shape), _const_spec(wl.shape),
                  _const_spec(wg.shape)],
        out_specs=[pl.BlockSpec((tm, w), lambda i: (i, 0)) for w, _ in outs],
        out_shape=[jax.ShapeDtypeStruct((n, w), dt) for w, dt in outs],
        compiler_params=_params("parallel"),
        name="inproj",
    )(x2, gain, sc, sh, wq, wr, wl, wg)


def _sb_attn_kernel(q_ref, k_ref, v_ref, o_ref, *, scale):
    t = q_ref.shape[0]
    qi = pl.program_id(2)
    row = lax.broadcasted_iota(jnp.int32, (t, t), 0)
    col = lax.broadcasted_iota(jnp.int32, (t, t), 1)
    tri = jnp.where(row > col, 1.0, 0.0).astype(BF16)
    strict = col < row
    lane = lax.broadcasted_iota(jnp.int32, (t, LANES), 1)
    first = lane < HEAD_DIM
    q2 = q_ref[...].astype(F32) * scale
    qs = (jnp.where(first, q2, 0.0).astype(BF16), jnp.where(first, 0.0, q2).astype(BF16))

    def block(kb, carry, masked):
        start = pl.multiple_of(kb * t, t)
        k2 = k_ref[pl.ds(start, t), :]
        v2 = v_ref[pl.ds(start, t), :]
        new = []
        for hh in range(2):
            acc, run = carry[hh]
            z = lax.dot_general(qs[hh], k2, (((1,), (1,)), ((), ())), preferred_element_type=F32)
            soft = jnp.log(1.0 + jnp.exp(-jnp.abs(z)))
            log_keep = -(jnp.maximum(z, 0.0) + soft)
            log_beta = log_keep + z
            if masked:
                log_keep = jnp.where(strict, log_keep, 0.0)
            later = jnp.dot(log_keep.astype(BF16), tri, preferred_element_type=F32)
            att = jnp.exp(log_beta + later + run)
            if masked:
                att = jnp.where(strict, att, 0.0)
            acc = acc + jnp.dot(att.astype(BF16), v2, preferred_element_type=F32)
            run = run + jnp.sum(log_keep, axis=-1, keepdims=True)
            new.append((acc, run))
        return tuple(new)

    zero = (jnp.zeros((t, LANES), F32), jnp.zeros((t, 1), F32))
    carry = block(qi, (zero, zero), True)
    carry = lax.fori_loop(0, qi, lambda i, c: block(qi - 1 - i, c, False), carry)
    o_ref[...] = jnp.where(first, carry[0][0], carry[1][0]).astype(o_ref.dtype)


def _sb_attention(qkv, batch, seq, width):
    n = qkv.shape[0]
    t = min(T_ATT, seq)
    n_pairs = width // LANES
    nq = seq // t
    kernel = functools.partial(_sb_attn_kernel, scale=1.0 / math.sqrt(HEAD_DIM))
    return pl.pallas_call(
        kernel,
        grid=(batch, n_pairs, nq),
        in_specs=[pl.BlockSpec((t, LANES), lambda b, p, i: (b * nq + i, p)),
                  pl.BlockSpec((seq, LANES), lambda b, p, i: (b, n_pairs + p)),
                  pl.BlockSpec((seq, LANES), lambda b, p, i: (b, 2 * n_pairs + p))],
        out_specs=pl.BlockSpec((t, LANES), lambda b, p, i: (b * nq + i, p)),
        out_shape=jax.ShapeDtypeStruct((n, width), BF16),
        compiler_params=_params("parallel", "parallel", "parallel"),
        name="sb_attn",
    )(qkv, qkv, qkv)


def _shift_rows(p, halo_row):
    rolled = pltpu.roll(p, 1, 0)
    row = lax.broadcasted_iota(jnp.int32, p.shape, 0)
    return jnp.where(row == 0, halo_row, rolled)


def _rw_prep_kernel(rkv_ref, rkv_halo_ref, lora_ref, lora_halo_ref, mu_rkv_ref, mu_lora_ref,
                    w0_ref, wd_ref, a0_ref, wa_ref, wg_ref, kk_ref, ka_ref, rk_ref, hsum_ref,
                    r_out, k_out, v_out, lw_out, a_out, b_out, bonus_out, g_out,
                    *, tiles_per_batch):
    i = pl.program_id(0)
    keep = jnp.where(i % tiles_per_batch == 0, 0.0, 1.0)
    w = w0_ref.shape[1]

    p = rkv_ref[...]
    prev = _shift_rows(p, rkv_halo_ref[SUBLANES - 1:SUBLANES, :] * keep)
    xs = p + mu_rkv_ref[...] * (prev - p)
    pl_ = lora_ref[...]
    prevl = _shift_rows(pl_, lora_halo_ref[SUBLANES - 1:SUBLANES, :] * keep)
    xl = pl_ + mu_lora_ref[...] * (prevl - pl_)

    r = xs[:, :w]
    k = xs[:, w:2 * w]
    v = xs[:, 2 * w:]
    dec_in = w0_ref[...] + _hdot(jnp.tanh(xl), wd_ref[...])
    neg = -dec_in
    softplus = jnp.maximum(neg, 0.0) + jnp.log(1.0 + jnp.exp(-jnp.abs(neg)))
    wlog = -softplus - 0.5
    lw_out[...] = -jnp.exp(wlog)
    a = _sigmoid(a0_ref[...] + _hdot(xl, wa_ref[...]))
    g_out[...] = _hdot(_sigmoid(xl), wg_ref[...])
    kk = k * kk_ref[...]
    k2 = k * (1.0 + (a - 1.0) * ka_ref[...])
    hsum = hsum_ref[...]
    norm = jnp.sqrt(_hdot(kk * kk, hsum))
    kkn = kk / jnp.maximum(norm, L2_EPS)
    r_out[...] = r
    k_out[...] = k2
    v_out[...] = v
    a_out[...] = -kkn
    b_out[...] = kkn * a
    bonus_out[...] = _hdot(r * k2 * rk_ref[...], hsum) * v


def _rw_prep(rkv, lora, mu_rkv, mu_lora, w0, wd_pad, a0, wa_pad, wg_pad, k_k, k_a, r_k, seq):
    n, w3 = rkv.shape
    w = w3 // 3
    wl = lora.shape[1]
    tm = min(TM_PREP, seq)
    tpb = seq // tm
    hb = tm // SUBLANES
    head = jnp.arange(w) // HEAD_DIM
    hsum = (head[:, None] == head[None, :]).astype(F32)
    kernel = functools.partial(_rw_prep_kernel, tiles_per_batch=tpb)
    row = lambda i: (i, 0)
    halo = lambda i: (jnp.maximum(i * hb - 1, 0), 0)
    vec = lambda a: a.reshape(1, -1)
    return pl.pallas_call(
        kernel,
        grid=(n // tm,),
        in_specs=[pl.BlockSpec((tm, w3), row), pl.BlockSpec((SUBLANES, w3), halo),
                  pl.BlockSpec((tm, wl), row), pl.BlockSpec((SUBLANES, wl), halo),
                  _const_spec((1, w3)), _const_spec((1, wl)),
                  _const_spec((1, w)), _const_spec((wl, w)), _const_spec((1, w)),
                  _const_spec((wl, w)), _const_spec((wl, w)),
                  _const_spec((1, w)), _const_spec((1, w)), _const_spec((1, w)),
                  _const_spec((w, w))],
        out_specs=[pl.BlockSpec((tm, w), row)] * 8,
        out_shape=[jax.ShapeDtypeStruct((n, w), F32)] * 8,
        compiler_params=_params("parallel"),
        name="rw_prep",
    )(rkv, rkv, lora, lora, vec(mu_rkv), vec(mu_lora), vec(w0), wd_pad, vec(a0), wa_pad, wg_pad,
      vec(k_k), vec(k_a), vec(r_k), hsum)


def _rw_scan_kernel(r_ref, k_ref, v_ref, lw_ref, a_ref, b_ref, bonus_ref, g_ref, lnw_ref, lnb_ref,
                    o_ref, state_ref):
    c = r_ref.shape[0]
    n_heads = r_ref.shape[1] // HEAD_DIM

    @pl.when(pl.program_id(1) == 0)
    def _():
        state_ref[...] = jnp.zeros_like(state_ref)

    row = lax.broadcasted_iota(jnp.int32, (c, c), 0)
    col = lax.broadcasted_iota(jnp.int32, (c, c), 1)
    incl = col <= row
    strict = col < row
    tril = jnp.where(incl, 1.0, 0.0)
    eye = jnp.where(row == col, 1.0, 0.0)

    for h in range(n_heads):
        sl = slice(h * HEAD_DIM, (h + 1) * HEAD_DIM)
        r, k, v = r_ref[:, sl], k_ref[:, sl], v_ref[:, sl]
        lw, a, b = lw_ref[:, sl], a_ref[:, sl], b_ref[:, sl]
        s0 = state_ref[h]

        cum = _hdot(tril, lw)
        total = cum[c - 1:c, :]
        a_t = a * jnp.exp(cum - lw)
        r_t = r * jnp.exp(cum)
        inv = jnp.exp(-cum)
        b_t = b * inv
        k_t = k * inv
        tail = jnp.exp(total - cum)
        b_h = b * tail
        k_h = k * tail

        a_ab = jnp.where(strict, _hdot_nt(a_t, b_t), 0.0)
        a_ak = jnp.where(strict, _hdot_nt(a_t, k_t), 0.0)
        a_rb = jnp.where(incl, _hdot_nt(r_t, b_t), 0.0)
        a_rk = jnp.where(incl, _hdot_nt(r_t, k_t), 0.0)

        inv_m = eye + a_ab
        power = a_ab
        for _ in range(int(math.log2(c)) - 1):
            power = _hdot(power, power)
            inv_m = inv_m + _hdot(inv_m, power)

        rhs = _hdot_nt(a_t, s0) + _hdot(a_ak, v)
        u = _hdot(inv_m, rhs)
        y = _hdot_nt(r_t, s0) + _hdot(a_rb, u) + _hdot(a_rk, v)
        state_ref[h] = s0 * jnp.exp(total) + _hdot_tn(u, b_h) + _hdot_tn(v, k_h)

        mean = jnp.mean(y, axis=-1, keepdims=True)
        var = jnp.mean(jnp.square(y - mean), axis=-1, keepdims=True)
        yn = (y - mean) * lax.rsqrt(var + GN_EPS) * lnw_ref[:, sl] + lnb_ref[:, sl]
        o_ref[:, sl] = ((yn + bonus_ref[:, sl]) * g_ref[:, sl]).astype(o_ref.dtype)


def _rw_scan(r, k, v, lw, a, b, bonus, g, lnx_w, lnx_b, batch, seq):
    n, w = r.shape
    c = min(C_SCAN, seq)
    nc = seq // c
    row = pl.BlockSpec((c, w), lambda bi, ci: (bi * nc + ci, 0))
    return pl.pallas_call(
        _rw_scan_kernel,
        grid=(batch, nc),
        in_specs=[row] * 8 + [_const_spec((1, w)), _const_spec((1, w))],
        out_specs=row,
        out_shape=jax.ShapeDtypeStruct((n, w), BF16),
        scratch_shapes=[pltpu.VMEM((w // HEAD_DIM, HEAD_DIM, HEAD_DIM), F32)],
        compiler_params=_params("parallel", "arbitrary"),
        name="rw_scan",
    )(r, k, v, lw, a, b, bonus, g, lnx_w.reshape(1, w), lnx_b.reshape(1, w))


def _lane_min_index(mask, lane):
    return jnp.min(jnp.where(mask, lane, 4 * LANES), axis=-1, keepdims=True)


def _merge_kernel(x_ref, ysb_ref, yrw_ref, gl_ref, gb_ref, wsb_ref, wrw_ref, wout_ref,
                  g1_ref, gt1_ref, g2_ref, sc2_ref, sh2_ref, wrt_ref,
                  x1_ref, h2_ref, route_ref, *, n_groups, per_group):
    d = x_ref.shape[1]
    tm = x_ref.shape[0]
    gates = _sigmoid(gl_ref[...] + gb_ref[...])
    merged = (gates[:, :d] * jnp.dot(ysb_ref[...], wsb_ref[...], preferred_element_type=F32)
              + gates[:, d:] * jnp.dot(yrw_ref[...], wrw_ref[...], preferred_element_type=F32))
    y = jnp.dot(merged.astype(BF16), wout_ref[...], preferred_element_type=F32)
    x1 = x_ref[...] + gt1_ref[...] * (_rms(y) * g1_ref[...])
    x1_ref[...] = x1
    h2 = _rms(x1) * g2_ref[...] * (1.0 + sc2_ref[...]) + sh2_ref[...]
    for j in range(d // LANES):
        h2_ref[pl.ds(j, tm, stride=SUBLANES), :] = h2[:, j * LANES:(j + 1) * LANES]

    lg = _hdot(h2, wrt_ref[...])
    lane = lax.broadcasted_iota(jnp.int32, lg.shape, 1)
    neg_inf = -jnp.inf
    gmask = lane < n_groups
    gmax = jnp.max(jnp.where(gmask, lg, neg_inf), axis=-1, keepdims=True)
    gsel = _lane_min_index(gmask & (lg == gmax), lane)
    gsum = jnp.sum(jnp.where(gmask, jnp.exp(lg - gmax), 0.0), axis=-1, keepdims=True)
    group_w = 1.0 / gsum
    lo = n_groups + gsel * per_group
    emask = (lane >= lo) & (lane < lo + per_group)
    m1 = jnp.max(jnp.where(emask, lg, neg_inf), axis=-1, keepdims=True)
    i1 = _lane_min_index(emask & (lg == m1), lane)
    rest = emask & (lane != i1)
    m2 = jnp.max(jnp.where(rest, lg, neg_inf), axis=-1, keepdims=True)
    i2 = _lane_min_index(rest & (lg == m2), lane)
    psum = jnp.sum(jnp.where(emask, jnp.exp(lg - m1), 0.0), axis=-1, keepdims=True)
    p1 = 1.0 / psum
    p2 = jnp.exp(m2 - m1) / psum
    w1 = group_w * (p1 / (p1 + p2))
    w2 = group_w * (p2 / (p1 + p2))
    e1 = (i1 - n_groups).astype(F32)
    e2 = (i2 - n_groups).astype(F32)
    route_ref[...] = jnp.where(lane == 0, e1, jnp.where(lane == 1, e2,
                               jnp.where(lane == 2, w1, jnp.where(lane == 3, w2, 0.0))))


def _merge(x2, ysb, yrw, gate_logits, gate_bias, wsb, wrw, wout, g1, gt1, g2, sc2, sh2, wrt,
           seq, n_groups, per_group):
    n, d = x2.shape
    tm = min(TM_MERGE, seq)
    tpb = seq // tm
    w = ysb.shape[1]
    row = lambda i: (i, 0)
    kernel = functools.partial(_merge_kernel, n_groups=n_groups, per_group=per_group)
    return pl.pallas_call(
        kernel,
        grid=(n // tm,),
        in_specs=[pl.BlockSpec((tm, d), row), pl.BlockSpec((tm, w), row), pl.BlockSpec((tm, w), row),
                  pl.BlockSpec((tm, 2 * d), row), _const_spec((1, 2 * d)),
                  _const_spec(wsb.shape), _const_spec(wrw.shape), _const_spec(wout.shape),
                  _const_spec((1, d)), _batch_spec(d, tpb), _const_spec((1, d)),
                  _batch_spec(d, tpb), _batch_spec(d, tpb), _const_spec(wrt.shape)],
        out_specs=[pl.BlockSpec((tm, d), row),
                   pl.BlockSpec((tm * SUBLANES, LANES), row),
                   pl.BlockSpec((tm, LANES), row)],
        out_shape=[jax.ShapeDtypeStruct((n, d), F32),
                   jax.ShapeDtypeStruct((n * SUBLANES, LANES), F32),
                   jax.ShapeDtypeStruct((n, LANES), F32)],
        compiler_params=_params("parallel"),
        name="merge",
    )(x2, ysb, yrw, gate_logits, gate_bias.reshape(1, -1), wsb, wrw, wout, g1, gt1, g2, sc2, sh2, wrt)


def _row_gather(src_hbm, idx_ref, base, dst, sem, n_rows):
    def body(r, _):
        tok = idx_ref[base + r]
        pltpu.make_async_copy(src_hbm.at[tok], dst.at[pl.ds(pl.multiple_of(r * SUBLANES, SUBLANES), SUBLANES), :],
                              sem).start()
        return 0
    lax.fori_loop(0, n_rows, body, 0)


def _row_gather_wait(src_hbm, dst, sem, n_rows):
    def body(r, _):
        pltpu.make_async_copy(src_hbm.at[0], dst.at[pl.ds(pl.multiple_of(r * SUBLANES, SUBLANES), SUBLANES), :],
                              sem).wait()
        return 0
    lax.fori_loop(0, n_rows, body, 0)


def _expert_kernel(tile_expert_ref, tile_valid_ref, row_tok_ref,
                   h2_hbm, roww_ref, w1_ref, w3_ref, wd_ref,
                   out_ref, gbuf, sems):
    tm = roww_ref.shape[0]
    d = w1_ref.shape[0]
    j = pl.program_id(0)
    nt = pl.num_programs(0)
    slot = j % 2

    def start(tile, s):
        @pl.when(tile_valid_ref[tile] == 1)
        def _():
            _row_gather(h2_hbm, row_tok_ref, tile * tm, gbuf.at[s], sems.at[s], tm)

    @pl.when(j == 0)
    def _():
        start(0, 0)

    @pl.when(j + 1 < nt)
    def _():
        start(j + 1, 1 - slot)

    @pl.when(tile_valid_ref[j] == 1)
    def _():
        _row_gather_wait(h2_hbm, gbuf.at[slot], sems.at[slot], tm)
        cur = gbuf.at[slot]
        x = jnp.concatenate([cur[pl.ds(c, tm, stride=SUBLANES), :].astype(BF16)
                             for c in range(d // LANES)], axis=1)
        up1 = jnp.dot(x, w1_ref[...], preferred_element_type=F32)
        up3 = jnp.dot(x, w3_ref[...], preferred_element_type=F32)
        hid = up1 * _sigmoid(up1) * up3
        y = jnp.dot(hid.astype(BF16), wd_ref[...], preferred_element_type=F32) * roww_ref[...]
        for c in range(d // LANES):
            out_ref[pl.ds(c, tm, stride=SUBLANES), :] = y[:, c * LANES:(c + 1) * LANES]

    @pl.when(tile_valid_ref[j] == 0)
    def _():
        out_ref[...] = jnp.zeros_like(out_ref)


def _experts(h2_tiles, tile_expert, tile_valid, row_tok, row_w, w1, w3, wd):
    n_tiles = tile_expert.shape[0]
    tm = TM_EXP
    _, d, de = w1.shape
    rows = n_tiles * tm
    grid_spec = pltpu.PrefetchScalarGridSpec(
        num_scalar_prefetch=3,
        grid=(n_tiles,),
        in_specs=[pl.BlockSpec(memory_space=pl.ANY),
                  pl.BlockSpec((tm, 1), lambda j, te, tv, rt: (j, 0)),
                  pl.BlockSpec((None, d, de), lambda j, te, tv, rt: (te[j], 0, 0)),
                  pl.BlockSpec((None, d, de), lambda j, te, tv, rt: (te[j], 0, 0)),
                  pl.BlockSpec((None, de, d), lambda j, te, tv, rt: (te[j], 0, 0))],
        out_specs=pl.BlockSpec((tm * SUBLANES, LANES), lambda j, te, tv, rt: (j, 0)),
        scratch_shapes=[pltpu.VMEM((2, tm * SUBLANES, LANES), F32),
                        pltpu.SemaphoreType.DMA((2,))],
    )
    return pl.pallas_call(
        _expert_kernel,
        grid_spec=grid_spec,
        out_shape=jax.ShapeDtypeStruct((rows * SUBLANES, LANES), F32),
        compiler_params=_params("arbitrary"),
        name="experts",
    )(tile_expert, tile_valid, row_tok, h2_tiles, row_w.reshape(rows, 1), w1, w3, wd)


def _combine_kernel(pos_ref,
                    ye_hbm, x1_ref, g3_ref, gt2_ref,
                    out_ref, gbuf, sems, *, n_tokens):
    tm, d = x1_ref.shape
    i = pl.program_id(0)
    nt = pl.num_programs(0)
    slot = i % 2

    def start(tile, s):
        for k in range(TOP_K_IN_GROUP):
            _row_gather(ye_hbm, pos_ref, k * n_tokens + tile * tm, gbuf.at[s, k], sems.at[s], tm)

    @pl.when(i == 0)
    def _():
        start(0, 0)

    @pl.when(i + 1 < nt)
    def _():
        start(i + 1, 1 - slot)

    for k in range(TOP_K_IN_GROUP):
        _row_gather_wait(ye_hbm, gbuf.at[slot, k], sems.at[slot], tm)
    y = jnp.concatenate(
        [sum(gbuf[slot, k, pl.ds(c, tm, stride=SUBLANES), :] for k in range(TOP_K_IN_GROUP))
         for c in range(d // LANES)], axis=1)
    out_ref[...] = x1_ref[...] + gt2_ref[...] * (_rms(y) * g3_ref[...])


def _combine(ye_tiles, pos, x1, g3, gt2, seq):
    n, d = x1.shape
    tm = min(TM_COMB, seq)
    tpb = seq // tm
    kernel = functools.partial(_combine_kernel, n_tokens=n)
    grid_spec = pltpu.PrefetchScalarGridSpec(
        num_scalar_prefetch=1,
        grid=(n // tm,),
        in_specs=[pl.BlockSpec(memory_space=pl.ANY),
                  pl.BlockSpec((tm, d), lambda i, p: (i, 0)),
                  pl.BlockSpec((1, d), lambda i, p: (0, 0)),
                  pl.BlockSpec((None, 1, d), lambda i, p: (i // tpb, 0, 0))],
        out_specs=pl.BlockSpec((tm, d), lambda i, p: (i, 0)),
        scratch_shapes=[pltpu.VMEM((2, TOP_K_IN_GROUP, tm * SUBLANES, LANES), F32),
                        pltpu.SemaphoreType.DMA((2,))],
    )
    return pl.pallas_call(
        kernel,
        grid_spec=grid_spec,
        out_shape=jax.ShapeDtypeStruct((n, d), F32),
        compiler_params=_params("arbitrary"),
        name="combine",
    )(pos, ye_tiles, x1, g3, gt2)


def _plan_rows(route, n_experts, tm):
    n = route.shape[0]
    pairs = TOP_K_IN_GROUP * n
    expert = jnp.concatenate([route[:, k] for k in range(TOP_K_IN_GROUP)]).astype(jnp.int32)
    weight = jnp.concatenate([route[:, TOP_K_IN_GROUP + k] for k in range(TOP_K_IN_GROUP)])
    token = jnp.tile(jnp.arange(n, dtype=jnp.int32), TOP_K_IN_GROUP)
    onehot = (expert[:, None] == jnp.arange(n_experts, dtype=jnp.int32)[None, :]).astype(jnp.int32)
    rank = jnp.take_along_axis(jnp.cumsum(onehot, axis=0), expert[:, None], axis=1)[:, 0] - 1
    counts = jnp.sum(onehot, axis=0)
    padded = -(-counts // tm) * tm
    ends = jnp.cumsum(padded)
    offsets = ends - padded
    pos = offsets[expert] + rank
    n_tiles = pairs // tm + n_experts
    rows = n_tiles * tm
    row_tok = jnp.zeros((rows,), jnp.int32).at[pos].set(token)
    row_w = jnp.zeros((rows,), F32).at[pos].set(weight)
    tile_start = jnp.arange(n_tiles, dtype=jnp.int32) * tm
    tile_expert = jnp.minimum(jnp.searchsorted(ends, tile_start, side="right"), n_experts - 1).astype(jnp.int32)
    tile_valid = (tile_start < ends[-1]).astype(jnp.int32)
    return tile_expert, tile_valid, row_tok, row_w, pos.astype(jnp.int32)


def _layer(x, c, w_mod, b_mod, norm_gains, w_in, gate_bias, shift_mu, w0, w_decay_up, a0, w_iclr_up,
           w_gate_up, k_k, k_a, r_k, lnx_w, lnx_b, w_o_sb, w_o_rw, w_out, w_group, w_router,
           w_up1, w_up3, w_down):
    batch, seq, d = x.shape
    n = batch * seq
    sb_width = w_o_sb.shape[0]
    rw_width = w_o_rw.shape[0]
    decay_lora, iclr_lora, gate_lora = w_decay_up.shape[0], w_iclr_up.shape[0], w_gate_up.shape[0]
    lora = decay_lora + iclr_lora + gate_lora
    lora_pad = -(-lora // LANES) * LANES
    n_groups = w_group.shape[1]
    n_experts = w_router.shape[1]
    per_group = n_experts // n_groups

    mod = _modulation(c, w_mod, b_mod)
    sh1, sc1, gt1, sh2, sc2, gt2 = [m.reshape(batch, 1, d) for m in jnp.split(mod, 6, axis=-1)]
    gains = norm_gains.reshape(4, 1, d)

    o1 = 3 * sb_width
    o2 = o1 + 3 * rw_width
    o3 = o2 + lora
    wq = w_in[:, :o1].astype(BF16)
    wr = w_in[:, o1:o2].astype(BF16)
    wl = jnp.zeros((d, lora_pad), BF16).at[:, :lora].set(w_in[:, o2:o3].astype(BF16))
    wg = w_in[:, o3:].astype(BF16)
    x2 = x.reshape(n, d)
    qkv, rkv, lora_cols, gate_logits = _inproj(x2, gains[0], sc1, sh1, wq, wr, wl, wg, seq)

    y_sb = _sb_attention(qkv, batch, seq, sb_width)

    mu_rkv = shift_mu[:3 * rw_width]
    mu_lora = jnp.zeros((lora_pad,), F32).at[:lora].set(shift_mu[3 * rw_width:])
    l1 = decay_lora
    l2 = l1 + iclr_lora
    wd_pad = jnp.zeros((lora_pad, rw_width), F32).at[:l1].set(w_decay_up)
    wa_pad = jnp.zeros((lora_pad, rw_width), F32).at[l1:l2].set(w_iclr_up)
    wg_pad = jnp.zeros((lora_pad, rw_width), F32).at[l2:lora].set(w_gate_up)
    r, k, v, lw, a, b, bonus, g = _rw_prep(rkv, lora_cols, mu_rkv, mu_lora, w0, wd_pad, a0, wa_pad,
                                           wg_pad, k_k, k_a, r_k.reshape(-1), seq)
    y_rw = _rw_scan(r, k, v, lw, a, b, bonus, g, lnx_w, lnx_b, batch, seq)

    wrt = jnp.zeros((d, LANES), F32).at[:, :n_groups].set(w_group)
    wrt = wrt.at[:, n_groups:n_groups + n_experts].set(w_router)
    x1, h2_tiles, route = _merge(x2, y_sb, y_rw, gate_logits, gate_bias, w_o_sb.astype(BF16),
                                 w_o_rw.astype(BF16), w_out.astype(BF16), gains[1], gt1, gains[2],
                                 sc2, sh2, wrt, seq, n_groups, per_group)

    tile_expert, tile_valid, row_tok, row_w, pos = _plan_rows(route, n_experts, TM_EXP)
    ye = _experts(h2_tiles.reshape(n, SUBLANES, LANES), tile_expert, tile_valid, row_tok, row_w,
                  w_up1.astype(BF16), w_up3.astype(BF16), w_down.astype(BF16))
    out = _combine(ye.reshape(-1, SUBLANES, LANES), pos, x1, gains[3], gt2, seq)
    return out.reshape(batch, seq, d)


def kernel(x, c, w_mod, b_mod, norm_gains, w_in, gate_bias, shift_mu, w0, w_decay_up, a0, w_iclr_up,
           w_gate_up, k_k, k_a, r_k, lnx_w, lnx_b, w_o_sb, w_o_rw, w_out, w_group, w_router,
           w_up1, w_up3, w_down):
    for l in range(w_mod.shape[0]):
        x = _layer(x, c, w_mod[l], b_mod[l], norm_gains[l], w_in[l], gate_bias[l], shift_mu[l], w0[l],
                   w_decay_up[l], a0[l], w_iclr_up[l], w_gate_up[l], k_k[l], k_a[l], r_k[l], lnx_w[l],
                   lnx_b[l], w_o_sb[l], w_o_rw[l], w_out[l], w_group[l], w_router[l], w_up1[l],
                   w_up3[l], w_down[l])
    return x
```

```python
import functools
import math

import jax
import jax.numpy as jnp
from jax import lax
from jax.experimental import pallas as pl
from jax.experimental.pallas import tpu as pltpu

F32 = jnp.float32
BF16 = jnp.bfloat16
HIGHEST = lax.Precision.HIGHEST

HEAD_DIM = 64
TOP_K_IN_GROUP = 2
NORM_EPS = 1e-6
GN_EPS = 64e-5
L2_EPS = 1e-12

LANES = 128
SUBLANES = 8
VMEM_LIMIT_BYTES = 56 * 1024 * 1024

TM_PROJ = 512
T_ATT = 256
ATT_UNROLL = 4
C_SCAN = 64
TM_CHUNK = 128
TM_MERGE = 256
TM_EXP = 256
TM_COMB = 256
GATHER_UNROLL = 8

_NN = ((1,), (0,))
_NT = ((1,), (1,))
_TN = ((0,), (0,))


def _params(*sem):
    return pltpu.CompilerParams(dimension_semantics=sem, vmem_limit_bytes=VMEM_LIMIT_BYTES)


def _hdot(a, b):
    return jnp.dot(a, b, precision=HIGHEST, preferred_element_type=F32)


def _split(x):
    hi = x.astype(BF16)
    return hi, (x - hi.astype(F32)).astype(BF16)


def _mm3(a, b, dims=_NN):
    a_hi, a_lo = _split(a)
    b_hi, b_lo = _split(b)
    dg = lambda p, q: lax.dot_general(p, q, (dims, ((), ())), preferred_element_type=F32)
    return dg(a_hi, b_hi) + (dg(a_lo, b_hi) + dg(a_hi, b_lo))


def _mm01(a, b, exact):
    if exact == "a":
        hi, lo = _split(b)
        return jnp.dot(a, hi, preferred_element_type=F32) + jnp.dot(a, lo, preferred_element_type=F32)
    hi, lo = _split(a)
    return jnp.dot(hi, b, preferred_element_type=F32) + jnp.dot(lo, b, preferred_element_type=F32)


def _rms(x):
    return x * lax.rsqrt(jnp.mean(x * x, axis=-1, keepdims=True) + NORM_EPS)


def _sigmoid(x):
    return 1.0 / (1.0 + jnp.exp(-x))


def _mod_kernel(c_ref, w_ref, b_ref, o_ref):
    c = c_ref[...]
    o_ref[...] = _hdot(c * _sigmoid(c), w_ref[...]) + b_ref[...]


def _modulation(c, w_mod, b_mod):
    b, d = c.shape
    n_out = w_mod.shape[1]
    rows = -(-b // SUBLANES) * SUBLANES
    c_pad = jnp.zeros((rows, d), F32).at[:b].set(c)
    out = pl.pallas_call(
        _mod_kernel,
        grid=(n_out // d,),
        in_specs=[pl.BlockSpec((rows, d), lambda j: (0, 0)),
                  pl.BlockSpec((d, d), lambda j: (0, j)),
                  pl.BlockSpec((1, d), lambda j: (0, j))],
        out_specs=pl.BlockSpec((rows, d), lambda j: (0, j)),
        out_shape=jax.ShapeDtypeStruct((rows, n_out), F32),
        compiler_params=_params("parallel"),
        name="mod",
    )(c_pad, w_mod, b_mod.reshape(1, n_out))
    return out[:b]


def _inproj_kernel(x_ref, g_ref, sc_ref, sh_ref, wq_ref, wr_ref, wl_ref, wg_ref,
                   qkv_ref, rkv_ref, lora_ref, gate_ref):
    h = _rms(x_ref[...]) * g_ref[...] * (1.0 + sc_ref[...]) + sh_ref[...]
    hb = h.astype(BF16)
    qkv_ref[...] = jnp.dot(hb, wq_ref[...], preferred_element_type=F32).astype(qkv_ref.dtype)
    rkv_ref[...] = jnp.dot(hb, wr_ref[...], preferred_element_type=F32)
    lora_ref[...] = jnp.dot(hb, wl_ref[...], preferred_element_type=F32)
    gate_ref[...] = jnp.dot(hb, wg_ref[...], preferred_element_type=F32)


def _const_spec(shape):
    return pl.BlockSpec(shape, lambda *_: (0,) * len(shape))


def _batch_spec(d, tiles_per_batch):
    return pl.BlockSpec((None, 1, d), lambda i: (i // tiles_per_batch, 0, 0))


def _inproj(x2, gain, sc, sh, wq, wr, wl, wg, seq):
    n, d = x2.shape
    tm = min(TM_PROJ, seq)
    tpb = seq // tm
    outs = [(wq.shape[1], BF16), (wr.shape[1], F32), (wl.shape[1], F32), (wg.shape[1], F32)]
    return pl.pallas_call(
        _inproj_kernel,
        grid=(n // tm,),
        in_specs=[pl.BlockSpec((tm, d), lambda i: (i, 0)),
                  _const_spec((1, d)), _batch_spec(d, tpb), _batch_spec(d, tpb),
                  _const_spec(wq.shape), _const_spec(wr.shape), _const_spec(wl.shape),
                  _const_spec(wg.shape)],
        out_specs=[pl.BlockSpec((tm, w), lambda i: (i, 0)) for w, _ in outs],
        out_shape=[jax.ShapeDtypeStruct((n, w), dt) for w, dt in outs],
        compiler_params=_params("parallel"),
        name="inproj",
    )(x2, gain, sc, sh, wq, wr, wl, wg)


def _sb_attn_kernel(q_ref, k_ref, v_ref, o_ref, acc_ref, *, scale):
    t = q_ref.shape[0]
    qi = pl.program_id(2)
    row = lax.broadcasted_iota(jnp.int32, (t, t), 0)
    col = lax.broadcasted_iota(jnp.int32, (t, t), 1)
    tri = jnp.where(row > col, 1.0, 0.0).astype(BF16)
    strict = col < row
    lane = lax.broadcasted_iota(jnp.int32, (t, LANES), 1)
    first = lane < HEAD_DIM
    q2 = q_ref[...].astype(F32) * (scale * math.log2(math.e))
    qs = (jnp.where(first, q2, 0.0).astype(BF16), jnp.where(first, 0.0, q2).astype(BF16))

    def blocks(kbs, runs, masked):
        k2 = [k_ref[pl.ds(pl.multiple_of(kb * t, t), t), :] for kb in kbs]
        v2 = [v_ref[pl.ds(pl.multiple_of(kb * t, t), t), :] for kb in kbs]
        z = [[lax.dot_general(qs[hh], k2[j], (_NT, ((), ())), preferred_element_type=F32) for hh in range(2)]
             for j in range(len(kbs))]
        runs = list(runs)
        for j in range(len(kbs)):
            for hh in range(2):
                zz = z[j][hh]
                neg_abs = pltpu.bitcast(pltpu.bitcast(zz, jnp.uint32) | jnp.uint32(0x80000000), F32)
                soft = jnp.log2(1.0 + jnp.exp2(neg_abs))
                log_beta = jnp.minimum(zz, 0.0) - soft
                log_keep = log_beta - zz
                if masked:
                    log_keep = jnp.where(strict, log_keep, 0.0)
                later = jnp.dot(log_keep.astype(BF16), tri, preferred_element_type=F32)
                att = jnp.exp2(log_beta + later + runs[hh])
                if masked:
                    att = jnp.where(strict, att, 0.0)
                acc_ref[hh] += jnp.dot(att.astype(BF16), v2[j], preferred_element_type=F32)
                runs[hh] = runs[hh] + jnp.sum(log_keep, axis=-1, keepdims=True)
        return tuple(runs)

    acc_ref[...] = jnp.zeros_like(acc_ref)
    zero = jnp.zeros((t, 1), F32)
    runs = blocks([qi], (zero, zero), True)
    rem = qi % ATT_UNROLL
    runs = lax.fori_loop(0, rem, lambda i, r: blocks([qi - 1 - i], r, False), runs)
    base = qi - 1 - rem

    def body(i, r):
        kb = base - ATT_UNROLL * i
        return blocks([kb - u for u in range(ATT_UNROLL)], r, False)

    lax.fori_loop(0, (qi - rem) // ATT_UNROLL, body, runs)
    o_ref[...] = jnp.where(first, acc_ref[0], acc_ref[1]).astype(o_ref.dtype)


def _sb_attention(qkv, batch, seq, width):
    n = qkv.shape[0]
    t = min(T_ATT, seq)
    n_pairs = width // LANES
    nq = seq // t
    kernel = functools.partial(_sb_attn_kernel, scale=1.0 / math.sqrt(HEAD_DIM))
    return pl.pallas_call(
        kernel,
        grid=(batch, n_pairs, nq),
        in_specs=[pl.BlockSpec((t, LANES), lambda b, p, i: (b * nq + i, p)),
                  pl.BlockSpec((seq, LANES), lambda b, p, i: (b, n_pairs + p)),
                  pl.BlockSpec((seq, LANES), lambda b, p, i: (b, 2 * n_pairs + p))],
        out_specs=pl.BlockSpec((t, LANES), lambda b, p, i: (b * nq + i, p)),
        out_shape=jax.ShapeDtypeStruct((n, width), BF16),
        scratch_shapes=[pltpu.VMEM((2, t, LANES), F32)],
        compiler_params=_params("parallel", "parallel", "parallel"),
        name="sb_attn",
    )(qkv, qkv, qkv)


def _shift_rows(p, halo_row):
    rolled = pltpu.roll(p, 1, 0)
    row = lax.broadcasted_iota(jnp.int32, p.shape, 0)
    return jnp.where(row == 0, halo_row, rolled)


def _rw_chunk_kernel(rkv_ref, rkv_halo_ref, lora_ref, lora_halo_ref, mu_rkv_ref, mu_lora_ref,
                     w0_ref, wd_ref, a0_ref, wa_ref, wg_ref, kk_ref, ka_ref, rk_ref, hsum_ref,
                     q_out, yv_out, bonus_out, g_out, m_out, n_out, *, tiles_per_batch, chunk):
    i = pl.program_id(0)
    keep = jnp.where(i % tiles_per_batch == 0, 0.0, 1.0)
    w = w0_ref.shape[1]
    tm = rkv_ref.shape[0]
    c = chunk
    hd = HEAD_DIM
    n_heads = w // hd

    p = rkv_ref[...]
    prev = _shift_rows(p, rkv_halo_ref[SUBLANES - 1:SUBLANES, :] * keep)
    xs = p + mu_rkv_ref[...] * (prev - p)
    pl_ = lora_ref[...]
    prevl = _shift_rows(pl_, lora_halo_ref[SUBLANES - 1:SUBLANES, :] * keep)
    xl = pl_ + mu_lora_ref[...] * (prevl - pl_)

    r = xs[:, :w]
    k = xs[:, w:2 * w]
    v = xs[:, 2 * w:]
    dec_in = w0_ref[...] + _hdot(jnp.tanh(xl), wd_ref[...])
    neg = -dec_in
    softplus = jnp.maximum(neg, 0.0) + jnp.log(1.0 + jnp.exp(-jnp.abs(neg)))
    lw = -jnp.exp(-softplus - 0.5)
    a_gate = _sigmoid(a0_ref[...] + _hdot(xl, wa_ref[...]))
    g_out[...] = _hdot(_sigmoid(xl), wg_ref[...])
    kk = k * kk_ref[...]
    k = k * (1.0 + (a_gate - 1.0) * ka_ref[...])
    hsum = hsum_ref[...]
    kkn = kk / jnp.maximum(jnp.sqrt(_mm01(kk * kk, hsum, "b")), L2_EPS)
    a = -kkn
    b = kkn * a_gate
    bonus_out[...] = _mm01(r * k * rk_ref[...], hsum, "b") * v

    row = lax.broadcasted_iota(jnp.int32, (c, c), 0)
    col = lax.broadcasted_iota(jnp.int32, (c, c), 1)
    incl = col <= row
    strict = col < row
    tril = jnp.where(incl, 1.0, 0.0).astype(BF16)
    eye = jnp.where(row == col, 1.0, 0.0)

    a_t, r_t, b_t, k_t, b_h, k_h, v_p, g_c = [], [], [], [], [], [], [], []
    for g in range(tm // c):
        rs = slice(g * c, (g + 1) * c)
        lw_c = lw[rs]
        cum = _mm01(tril, lw_c, "a")
        total = cum[c - 1:c, :]
        inv = jnp.exp(-cum)
        tail = jnp.exp(total - cum)
        full = {"a_t": a[rs] * jnp.exp(cum - lw_c), "r_t": r[rs] * jnp.exp(cum), "b_t": b[rs] * inv,
                "k_t": k[rs] * inv, "b_h": b[rs] * tail, "k_h": k[rs] * tail, "v": v[rs],
                "g_c": jnp.exp(total)}
        for h in range(n_heads):
            sl = slice(h * hd, (h + 1) * hd)
            a_t.append(full["a_t"][:, sl]); r_t.append(full["r_t"][:, sl])
            b_t.append(full["b_t"][:, sl]); k_t.append(full["k_t"][:, sl])
            b_h.append(full["b_h"][:, sl]); k_h.append(full["k_h"][:, sl])
            v_p.append(full["v"][:, sl]); g_c.append(full["g_c"][:, sl])
    n_prob = len(a_t)
    prob = range(n_prob)

    lhs = [jnp.concatenate([a_t[j], r_t[j]], axis=0) for j in prob]
    gram_b = [_mm3(lhs[j], b_t[j], _NT) for j in prob]
    gram_k = [_mm3(lhs[j], k_t[j], _NT) for j in prob]
    a_ab = [jnp.where(strict, gram_b[j][:c], 0.0) for j in prob]
    a_rb = [jnp.where(incl, gram_b[j][c:], 0.0) for j in prob]
    a_ak = [jnp.where(strict, gram_k[j][:c], 0.0) for j in prob]
    a_rk = [jnp.where(incl, gram_k[j][c:], 0.0) for j in prob]
    ak_v = [_mm3(a_ak[j], v_p[j]) for j in prob]
    rk_v = [_mm3(a_rk[j], v_p[j]) for j in prob]
    wu = [jnp.concatenate([a_t[j], ak_v[j]], axis=1) for j in prob]
    power = a_ab
    n_apply = max(1, math.ceil(math.log2(c)))
    for step in range(n_apply):
        wu = [wu[j] + _mm3(power[j], wu[j]) for j in prob]
        if step + 1 < n_apply:
            power = [_mm3(power[j], power[j]) for j in prob]
    qy = [_mm3(a_rb[j], wu[j]) + jnp.concatenate([r_t[j], rk_v[j]], axis=1) for j in prob]
    wu_b = [_mm3(wu[j], b_h[j], _TN) for j in prob]
    v_k = [_mm3(v_p[j], k_h[j], _TN) for j in prob]
    for j in prob:
        g, h = divmod(j, n_heads)
        rs = slice(g * c, (g + 1) * c)
        sl = slice(h * hd, (h + 1) * hd)
        q_out[rs, sl] = qy[j][:, :hd]
        yv_out[rs, sl] = qy[j][:, hd:]
        m_out[g, h] = wu_b[j][:hd] + eye * g_c[j]
        n_out[g, h] = wu_b[j][hd:] + v_k[j]


def _rw_chunk(rkv, lora, mu_rkv, mu_lora, w0, wd_pad, a0, wa_pad, wg_pad, k_k, k_a, r_k, hsum, seq):
    n, w3 = rkv.shape
    w = w3 // 3
    wl = lora.shape[1]
    c = min(C_SCAN, seq)
    tm = min(TM_CHUNK, seq)
    tpb = seq // tm
    cpt = tm // c
    hb = tm // SUBLANES
    n_heads = w // HEAD_DIM
    kernel = functools.partial(_rw_chunk_kernel, tiles_per_batch=tpb, chunk=c)
    row = lambda i: (i, 0)
    halo = lambda i: (jnp.maximum(i * hb - 1, 0), 0)
    vec = lambda a: a.reshape(1, -1)
    mat = pl.BlockSpec((cpt, n_heads, HEAD_DIM, HEAD_DIM), lambda i: (i, 0, 0, 0))
    mat_shape = jax.ShapeDtypeStruct((n // c, n_heads, HEAD_DIM, HEAD_DIM), F32)
    return pl.pallas_call(
        kernel,
        grid=(n // tm,),
        in_specs=[pl.BlockSpec((tm, w3), row), pl.BlockSpec((SUBLANES, w3), halo),
                  pl.BlockSpec((tm, wl), row), pl.BlockSpec((SUBLANES, wl), halo),
                  _const_spec((1, w3)), _const_spec((1, wl)),
                  _const_spec((1, w)), _const_spec((wl, w)), _const_spec((1, w)),
                  _const_spec((wl, w)), _const_spec((wl, w)),
                  _const_spec((1, w)), _const_spec((1, w)), _const_spec((1, w)),
                  _const_spec((w, w))],
        out_specs=[pl.BlockSpec((tm, w), row)] * 4 + [mat, mat],
        out_shape=[jax.ShapeDtypeStruct((n, w), F32)] * 4 + [mat_shape, mat_shape],
        compiler_params=_params("parallel"),
        name="rw_chunk",
    )(rkv, rkv, lora, lora, vec(mu_rkv), vec(mu_lora), vec(w0), wd_pad, vec(a0), wa_pad, wg_pad,
      vec(k_k), vec(k_a), vec(r_k), hsum)


def _rw_state_kernel(q_ref, yv_ref, bonus_ref, g_ref, m_ref, n_ref, lnw_ref, lnb_ref, hsum_ref,
                     o_ref, state_ref):
    n_batch, c, w = q_ref.shape
    hd = HEAD_DIM
    n_heads = w // hd

    @pl.when(pl.program_id(0) == 0)
    def _():
        state_ref[...] = jnp.zeros_like(state_ref)

    pairs = [(bi, h) for bi in range(n_batch) for h in range(n_heads)]
    state = [state_ref[bi, h] for bi, h in pairs]
    y = [_mm3(q_ref[bi, :, h * hd:(h + 1) * hd], state[j], _NT) + yv_ref[bi, :, h * hd:(h + 1) * hd]
         for j, (bi, h) in enumerate(pairs)]
    new_state = [_mm3(state[j], m_ref[bi, h]) + n_ref[bi, h] for j, (bi, h) in enumerate(pairs)]
    for j, (bi, h) in enumerate(pairs):
        state_ref[bi, h] = new_state[j]

    hsum = hsum_ref[...]
    for bi in range(n_batch):
        y_b = jnp.concatenate(y[bi * n_heads:(bi + 1) * n_heads], axis=1)
        mean = _mm01(y_b, hsum, "b") * (1.0 / hd)
        dev = y_b - mean
        var = _mm01(dev * dev, hsum, "b") * (1.0 / hd)
        yn = dev * lax.rsqrt(var + GN_EPS) * lnw_ref[...] + lnb_ref[...]
        o_ref[bi] = ((yn + bonus_ref[bi]) * g_ref[bi]).astype(o_ref.dtype)


def _rw_state(q, yv, bonus, g, m, n_mat, lnx_w, lnx_b, hsum, batch, seq):
    w = q.shape[1]
    c = min(C_SCAN, seq)
    nc = seq // c
    n_heads = w // HEAD_DIM
    rows = pl.BlockSpec((batch, c, w), lambda ci: (0, ci, 0))
    mats = pl.BlockSpec((batch, None, n_heads, HEAD_DIM, HEAD_DIM), lambda ci: (0, ci, 0, 0, 0))
    to3 = lambda a: a.reshape(batch, seq, w)
    to5 = lambda a: a.reshape(batch, nc, n_heads, HEAD_DIM, HEAD_DIM)
    out = pl.pallas_call(
        _rw_state_kernel,
        grid=(nc,),
        in_specs=[rows] * 4 + [mats, mats, _const_spec((1, w)), _const_spec((1, w)), _const_spec((w, w))],
        out_specs=rows,
        out_shape=jax.ShapeDtypeStruct((batch, seq, w), BF16),
        scratch_shapes=[pltpu.VMEM((batch, n_heads, HEAD_DIM, HEAD_DIM), F32)],
        compiler_params=_params("arbitrary"),
        name="rw_state",
    )(to3(q), to3(yv), to3(bonus), to3(g), to5(m), to5(n_mat), lnx_w.reshape(1, w), lnx_b.reshape(1, w), hsum)
    return out.reshape(batch * seq, w)


def _lane_min_index(mask, lane):
    return jnp.min(jnp.where(mask, lane, 4 * LANES), axis=-1, keepdims=True)


def _merge_kernel(x_ref, ysb_ref, yrw_ref, gl_ref, gb_ref, wsb_ref, wrw_ref, wout_ref,
                  g1_ref, gt1_ref, g2_ref, sc2_ref, sh2_ref, wrt_ref,
                  x1_ref, h2_ref, route_ref, *, n_groups, per_group):
    d = x_ref.shape[1]
    tm = x_ref.shape[0]
    gates = _sigmoid(gl_ref[...] + gb_ref[...])
    merged = (gates[:, :d] * jnp.dot(ysb_ref[...], wsb_ref[...], preferred_element_type=F32)
              + gates[:, d:] * jnp.dot(yrw_ref[...], wrw_ref[...], preferred_element_type=F32))
    y = jnp.dot(merged.astype(BF16), wout_ref[...], preferred_element_type=F32)
    x1 = x_ref[...] + gt1_ref[...] * (_rms(y) * g1_ref[...])
    x1_ref[...] = x1
    h2 = _rms(x1) * g2_ref[...] * (1.0 + sc2_ref[...]) + sh2_ref[...]
    for j in range(d // LANES):
        h2_ref[pl.ds(j, tm, stride=SUBLANES), :] = h2[:, j * LANES:(j + 1) * LANES]

    lg = _hdot(h2, wrt_ref[...])
    lane = lax.broadcasted_iota(jnp.int32, lg.shape, 1)
    neg_inf = -jnp.inf
    gmask = lane < n_groups
    gmax = jnp.max(jnp.where(gmask, lg, neg_inf), axis=-1, keepdims=True)
    gsel = _lane_min_index(gmask & (lg == gmax), lane)
    gsum = jnp.sum(jnp.where(gmask, jnp.exp(lg - gmax), 0.0), axis=-1, keepdims=True)
    group_w = 1.0 / gsum
    lo = n_groups + gsel * per_group
    emask = (lane >= lo) & (lane < lo + per_group)
    m1 = jnp.max(jnp.where(emask, lg, neg_inf), axis=-1, keepdims=True)
    i1 = _lane_min_index(emask & (lg == m1), lane)
    rest = emask & (lane != i1)
    m2 = jnp.max(jnp.where(rest, lg, neg_inf), axis=-1, keepdims=True)
    i2 = _lane_min_index(rest & (lg == m2), lane)
    psum = jnp.sum(jnp.where(emask, jnp.exp(lg - m1), 0.0), axis=-1, keepdims=True)
    p1 = 1.0 / psum
    p2 = jnp.exp(m2 - m1) / psum
    w1 = group_w * (p1 / (p1 + p2))
    w2 = group_w * (p2 / (p1 + p2))
    e1 = (i1 - n_groups).astype(F32)
    e2 = (i2 - n_groups).astype(F32)
    route_ref[...] = jnp.where(lane == 0, e1, jnp.where(lane == 1, e2,
                               jnp.where(lane == 2, w1, jnp.where(lane == 3, w2, 0.0))))


def _merge(x2, ysb, yrw, gate_logits, gate_bias, wsb, wrw, wout, g1, gt1, g2, sc2, sh2, wrt,
           seq, n_groups, per_group):
    n, d = x2.shape
    tm = min(TM_MERGE, seq)
    tpb = seq // tm
    w = ysb.shape[1]
    row = lambda i: (i, 0)
    kernel = functools.partial(_merge_kernel, n_groups=n_groups, per_group=per_group)
    return pl.pallas_call(
        kernel,
        grid=(n // tm,),
        in_specs=[pl.BlockSpec((tm, d), row), pl.BlockSpec((tm, w), row), pl.BlockSpec((tm, w), row),
                  pl.BlockSpec((tm, 2 * d), row), _const_spec((1, 2 * d)),
                  _const_spec(wsb.shape), _const_spec(wrw.shape), _const_spec(wout.shape),
                  _const_spec((1, d)), _batch_spec(d, tpb), _const_spec((1, d)),
                  _batch_spec(d, tpb), _batch_spec(d, tpb), _const_spec(wrt.shape)],
        out_specs=[pl.BlockSpec((tm, d), row),
                   pl.BlockSpec((tm * SUBLANES, LANES), row),
                   pl.BlockSpec((tm, LANES), row)],
        out_shape=[jax.ShapeDtypeStruct((n, d), F32),
                   jax.ShapeDtypeStruct((n * SUBLANES, LANES), F32),
                   jax.ShapeDtypeStruct((n, LANES), F32)],
        compiler_params=_params("parallel"),
        name="merge",
    )(x2, ysb, yrw, gate_logits, gate_bias.reshape(1, -1), wsb, wrw, wout, g1, gt1, g2, sc2, sh2, wrt)


def _tile_rows(ref, index):
    return ref.at[pl.ds(pl.multiple_of(index * SUBLANES, SUBLANES), SUBLANES), :]


def _row_gather(src_hbm, idx_ref, base, dst, sem, n_rows):
    def body(r8, _):
        for u in range(GATHER_UNROLL):
            r = r8 * GATHER_UNROLL + u
            pltpu.make_async_copy(_tile_rows(src_hbm, idx_ref[base + r]), _tile_rows(dst, r), sem).start()
        return 0
    lax.fori_loop(0, n_rows // GATHER_UNROLL, body, 0)


def _row_gather_wait(src_hbm, dst, sem):
    pltpu.make_async_copy(src_hbm.at[pl.ds(0, dst.shape[0]), :], dst, sem).wait()


def _expert_kernel(tile_expert_ref, tile_valid_ref, row_tok_ref,
                   h2_hbm, w1_ref, w3_ref, wd_ref,
                   out_ref, gbuf, sems):
    tm = out_ref.shape[0] // SUBLANES
    d = w1_ref.shape[0]
    j = pl.program_id(0)
    nt = pl.num_programs(0)
    slot = j % 2

    def start(tile, s):
        @pl.when(tile_valid_ref[tile] == 1)
        def _():
            _row_gather(h2_hbm, row_tok_ref, tile * tm, gbuf.at[s], sems.at[s], tm)

    @pl.when(j == 0)
    def _():
        start(0, 0)

    @pl.when(j + 1 < nt)
    def _():
        start(j + 1, 1 - slot)

    @pl.when(tile_valid_ref[j] == 1)
    def _():
        _row_gather_wait(h2_hbm, gbuf.at[slot], sems.at[slot])
        cur = gbuf.at[slot]
        x = jnp.concatenate([cur[pl.ds(c, tm, stride=SUBLANES), :].astype(BF16)
                             for c in range(d // LANES)], axis=1)
        up1 = jnp.dot(x, w1_ref[...], preferred_element_type=F32)
        up3 = jnp.dot(x, w3_ref[...], preferred_element_type=F32)
        hid = up1 * _sigmoid(up1) * up3
        y = jnp.dot(hid.astype(BF16), wd_ref[...], preferred_element_type=F32)
        for c in range(d // LANES):
            out_ref[pl.ds(c, tm, stride=SUBLANES), :] = y[:, c * LANES:(c + 1) * LANES]

    @pl.when(tile_valid_ref[j] == 0)
    def _():
        out_ref[...] = jnp.zeros_like(out_ref)


def _experts(h2_tiles, tile_expert, tile_valid, row_tok, w1, w3, wd):
    n_tiles = tile_expert.shape[0]
    tm = TM_EXP
    _, d, de = w1.shape
    rows = n_tiles * tm
    grid_spec = pltpu.PrefetchScalarGridSpec(
        num_scalar_prefetch=3,
        grid=(n_tiles,),
        in_specs=[pl.BlockSpec(memory_space=pl.ANY),
                  pl.BlockSpec((None, d, de), lambda j, te, tv, rt: (te[j], 0, 0)),
                  pl.BlockSpec((None, d, de), lambda j, te, tv, rt: (te[j], 0, 0)),
                  pl.BlockSpec((None, de, d), lambda j, te, tv, rt: (te[j], 0, 0))],
        out_specs=pl.BlockSpec((tm * SUBLANES, LANES), lambda j, te, tv, rt: (j, 0)),
        scratch_shapes=[pltpu.VMEM((2, tm * SUBLANES, LANES), F32),
                        pltpu.SemaphoreType.DMA((2,))],
    )
    return pl.pallas_call(
        _expert_kernel,
        grid_spec=grid_spec,
        out_shape=jax.ShapeDtypeStruct((rows * SUBLANES, LANES), F32),
        compiler_params=_params("arbitrary"),
        name="experts",
    )(tile_expert, tile_valid, row_tok, h2_tiles, w1, w3, wd)


def _combine_kernel(pos_ref,
                    ye_hbm, x1_ref, route_ref, g3_ref, gt2_ref,
                    out_ref, gbuf, sems, *, n_tokens):
    tm, d = x1_ref.shape
    i = pl.program_id(0)
    nt = pl.num_programs(0)
    slot = i % 2

    def start(tile, s):
        for k in range(TOP_K_IN_GROUP):
            _row_gather(ye_hbm, pos_ref, k * n_tokens + tile * tm, gbuf.at[s, k], sems.at[s], tm)

    @pl.when(i == 0)
    def _():
        start(0, 0)

    @pl.when(i + 1 < nt)
    def _():
        start(i + 1, 1 - slot)

    for k in range(TOP_K_IN_GROUP):
        _row_gather_wait(ye_hbm, gbuf.at[slot, k], sems.at[slot])
    wts = [route_ref[:, TOP_K_IN_GROUP + k:TOP_K_IN_GROUP + k + 1] for k in range(TOP_K_IN_GROUP)]
    y = jnp.concatenate(
        [sum(wts[k] * gbuf[slot, k, pl.ds(c, tm, stride=SUBLANES), :] for k in range(TOP_K_IN_GROUP))
         for c in range(d // LANES)], axis=1)
    out_ref[...] = x1_ref[...] + gt2_ref[...] * (_rms(y) * g3_ref[...])


def _combine(ye_tiles, pos, x1, route, g3, gt2, seq):
    n, d = x1.shape
    tm = min(TM_COMB, seq)
    tpb = seq // tm
    kernel = functools.partial(_combine_kernel, n_tokens=n)
    grid_spec = pltpu.PrefetchScalarGridSpec(
        num_scalar_prefetch=1,
        grid=(n // tm,),
        in_specs=[pl.BlockSpec(memory_space=pl.ANY),
                  pl.BlockSpec((tm, d), lambda i, p: (i, 0)),
                  pl.BlockSpec((tm, LANES), lambda i, p: (i, 0)),
                  pl.BlockSpec((1, d), lambda i, p: (0, 0)),
                  pl.BlockSpec((None, 1, d), lambda i, p: (i // tpb, 0, 0))],
        out_specs=pl.BlockSpec((tm, d), lambda i, p: (i, 0)),
        scratch_shapes=[pltpu.VMEM((2, TOP_K_IN_GROUP, tm * SUBLANES, LANES), F32),
                        pltpu.SemaphoreType.DMA((2,))],
    )
    return pl.pallas_call(
        kernel,
        grid_spec=grid_spec,
        out_shape=jax.ShapeDtypeStruct((n, d), F32),
        compiler_params=_params("arbitrary"),
        name="combine",
    )(pos, ye_tiles, x1, route, g3, gt2)


def _plan_rows(route, n_experts, tm):
    n = route.shape[0]
    pairs = TOP_K_IN_GROUP * n
    expert = jnp.concatenate([route[:, k] for k in range(TOP_K_IN_GROUP)]).astype(jnp.int32)
    token = jnp.tile(jnp.arange(n, dtype=jnp.int32), TOP_K_IN_GROUP)
    onehot = (expert[:, None] == jnp.arange(n_experts, dtype=jnp.int32)[None, :]).astype(jnp.int32)
    rank = jnp.sum(jnp.cumsum(onehot, axis=0) * onehot, axis=1) - 1
    counts = jnp.sum(onehot, axis=0)
    padded = -(-counts // tm) * tm
    ends = jnp.cumsum(padded)
    offsets = ends - padded
    pos = jnp.sum(onehot * offsets[None, :], axis=1) + rank
    n_tiles = pairs // tm + n_experts
    rows = n_tiles * tm
    row_tok = jnp.zeros((rows,), jnp.int32).at[pos].set(token)
    tile_start = jnp.arange(n_tiles, dtype=jnp.int32) * tm
    tile_expert = jnp.minimum(jnp.sum((ends[None, :] <= tile_start[:, None]).astype(jnp.int32), axis=1),
                              n_experts - 1)
    tile_valid = (tile_start < ends[-1]).astype(jnp.int32)
    return tile_expert, tile_valid, row_tok, pos.astype(jnp.int32)


def _layer(x, c, w_mod, b_mod, norm_gains, w_in, gate_bias, shift_mu, w0, w_decay_up, a0, w_iclr_up,
           w_gate_up, k_k, k_a, r_k, lnx_w, lnx_b, w_o_sb, w_o_rw, w_out, w_group, w_router,
           w_up1, w_up3, w_down):
    batch, seq, d = x.shape
    n = batch * seq
    sb_width = w_o_sb.shape[0]
    rw_width = w_o_rw.shape[0]
    decay_lora, iclr_lora, gate_lora = w_decay_up.shape[0], w_iclr_up.shape[0], w_gate_up.shape[0]
    lora = decay_lora + iclr_lora + gate_lora
    lora_pad = -(-lora // LANES) * LANES
    n_groups = w_group.shape[1]
    n_experts = w_router.shape[1]
    per_group = n_experts // n_groups

    mod = _modulation(c, w_mod, b_mod)
    sh1, sc1, gt1, sh2, sc2, gt2 = [m.reshape(batch, 1, d) for m in jnp.split(mod, 6, axis=-1)]
    gains = norm_gains.reshape(4, 1, d)

    o1 = 3 * sb_width
    o2 = o1 + 3 * rw_width
    o3 = o2 + lora
    wq = w_in[:, :o1].astype(BF16)
    wr = w_in[:, o1:o2].astype(BF16)
    wl = jnp.zeros((d, lora_pad), BF16).at[:, :lora].set(w_in[:, o2:o3].astype(BF16))
    wg = w_in[:, o3:].astype(BF16)
    x2 = x.reshape(n, d)
    qkv, rkv, lora_cols, gate_logits = _inproj(x2, gains[0], sc1, sh1, wq, wr, wl, wg, seq)

    y_sb = _sb_attention(qkv, batch, seq, sb_width)

    mu_rkv = shift_mu[:3 * rw_width]
    mu_lora = jnp.zeros((lora_pad,), F32).at[:lora].set(shift_mu[3 * rw_width:])
    l1 = decay_lora
    l2 = l1 + iclr_lora
    wd_pad = jnp.zeros((lora_pad, rw_width), F32).at[:l1].set(w_decay_up)
    wa_pad = jnp.zeros((lora_pad, rw_width), F32).at[l1:l2].set(w_iclr_up)
    wg_pad = jnp.zeros((lora_pad, rw_width), F32).at[l2:lora].set(w_gate_up)
    head = jnp.arange(rw_width) // HEAD_DIM
    hsum = (head[:, None] == head[None, :]).astype(BF16)
    q_c, yv_c, bonus, g, m_c, n_c = _rw_chunk(rkv, lora_cols, mu_rkv, mu_lora, w0, wd_pad, a0, wa_pad,
                                              wg_pad, k_k, k_a, r_k.reshape(-1), hsum, seq)
    y_rw = _rw_state(q_c, yv_c, bonus, g, m_c, n_c, lnx_w, lnx_b, hsum, batch, seq)

    wrt = jnp.zeros((d, LANES), F32).at[:, :n_groups].set(w_group)
    wrt = wrt.at[:, n_groups:n_groups + n_experts].set(w_router)
    x1, h2_tiles, route = _merge(x2, y_sb, y_rw, gate_logits, gate_bias, w_o_sb.astype(BF16),
                                 w_o_rw.astype(BF16), w_out.astype(BF16), gains[1], gt1, gains[2],
                                 sc2, sh2, wrt, seq, n_groups, per_group)

    tile_expert, tile_valid, row_tok, pos = _plan_rows(route, n_experts, TM_EXP)
    ye = _experts(h2_tiles, tile_expert, tile_valid, row_tok,
                  w_up1.astype(BF16), w_up3.astype(BF16), w_down.astype(BF16))
    out = _combine(ye, pos, x1, route, gains[3], gt2, seq)
    return out.reshape(batch, seq, d)


def kernel(x, c, w_mod, b_mod, norm_gains, w_in, gate_bias, shift_mu, w0, w_decay_up, a0, w_iclr_up,
           w_gate_up, k_k, k_a, r_k, lnx_w, lnx_b, w_o_sb, w_o_rw, w_out, w_group, w_router,
           w_up1, w_up3, w_down):
    for l in range(w_mod.shape[0]):
        x = _layer(x, c, w_mod[l], b_mod[l], norm_gains[l], w_in[l], gate_bias[l], shift_mu[l], w0[l],
                   w_decay_up[l], a0[l], w_iclr_up[l], w_gate_up[l], k_k[l], k_a[l], r_k[l], lnx_w[l],
                   lnx_b[l], w_o_sb[l], w_o_rw[l], w_out[l], w_group[l], w_router[l], w_up1[l],
                   w_up3[l], w_down[l])
    return x
```

```python
import functools
import math

import jax
import jax.numpy as jnp
from jax import lax
from jax.experimental import pallas as pl
from jax.experimental.pallas import tpu as pltpu

F32 = jnp.float32
BF16 = jnp.bfloat16
HIGHEST = lax.Precision.HIGHEST

HEAD_DIM = 64
TOP_K_IN_GROUP = 2
NORM_EPS = 1e-6
GN_EPS = 64e-5
L2_EPS = 1e-12

LANES = 128
SUBLANES = 8
VMEM_LIMIT_BYTES = 56 * 1024 * 1024

TM_PROJ = 512
T_ATT = 256
ATT_DEAD_LOG2 = -150.0
C_SCAN = 64
TM_CHUNK = 256
TM_MERGE = 512
TM_EXP = 256
TM_COMB = 256
GATHER_UNROLL = 8

_NN = ((1,), (0,))
_NT = ((1,), (1,))
_TN = ((0,), (0,))


def _params(*sem):
    return pltpu.CompilerParams(dimension_semantics=sem, vmem_limit_bytes=VMEM_LIMIT_BYTES)


def _hdot(a, b):
    return jnp.dot(a, b, precision=HIGHEST, preferred_element_type=F32)


def _split(x):
    hi = x.astype(BF16)
    return hi, (x - hi.astype(F32)).astype(BF16)


def _mm1(a, b, dims=_NN):
    return lax.dot_general(a.astype(BF16), b.astype(BF16), (dims, ((), ())), preferred_element_type=F32)


def _mm3(a, b, dims=_NN):
    a_hi, a_lo = _split(a)
    b_hi, b_lo = _split(b)
    dg = lambda p, q: lax.dot_general(p, q, (dims, ((), ())), preferred_element_type=F32)
    return dg(a_hi, b_hi) + (dg(a_lo, b_hi) + dg(a_hi, b_lo))


def _mm01(a, b, exact):
    if exact == "a":
        hi, lo = _split(b)
        return jnp.dot(a, hi, preferred_element_type=F32) + jnp.dot(a, lo, preferred_element_type=F32)
    hi, lo = _split(a)
    return jnp.dot(hi, b, preferred_element_type=F32) + jnp.dot(lo, b, preferred_element_type=F32)


def _rms(x):
    return x * lax.rsqrt(jnp.mean(x * x, axis=-1, keepdims=True) + NORM_EPS)


def _sigmoid(x):
    return 1.0 / (1.0 + jnp.exp(-x))


def _mod_kernel(c_ref, w_ref, b_ref, o_ref):
    c = c_ref[...]
    o_ref[...] = _hdot(c * _sigmoid(c), w_ref[...]) + b_ref[...]


def _modulation(c, w_mod, b_mod):
    b, d = c.shape
    n_out = w_mod.shape[1]
    rows = -(-b // SUBLANES) * SUBLANES
    c_pad = jnp.zeros((rows, d), F32).at[:b].set(c)
    out = pl.pallas_call(
        _mod_kernel,
        grid=(n_out // d,),
        in_specs=[pl.BlockSpec((rows, d), lambda j: (0, 0)),
                  pl.BlockSpec((d, d), lambda j: (0, j)),
                  pl.BlockSpec((1, d), lambda j: (0, j))],
        out_specs=pl.BlockSpec((rows, d), lambda j: (0, j)),
        out_shape=jax.ShapeDtypeStruct((rows, n_out), F32),
        compiler_params=_params("parallel"),
        name="mod",
    )(c_pad, w_mod, b_mod.reshape(1, n_out))
    return out[:b]


def _inproj_kernel(x_ref, g_ref, sc_ref, sh_ref, wq_ref, wr_ref, wl_ref, wg_ref,
                   qkv_ref, rkv_ref, lora_ref, gate_ref):
    h = _rms(x_ref[...]) * g_ref[...] * (1.0 + sc_ref[...]) + sh_ref[...]
    hb = h.astype(BF16)
    qkv_ref[...] = jnp.dot(hb, wq_ref[...], preferred_element_type=F32).astype(qkv_ref.dtype)
    rkv_ref[...] = jnp.dot(hb, wr_ref[...], preferred_element_type=F32)
    lora_ref[...] = jnp.dot(hb, wl_ref[...], preferred_element_type=F32)
    gate_ref[...] = jnp.dot(hb, wg_ref[...], preferred_element_type=F32)


def _const_spec(shape):
    return pl.BlockSpec(shape, lambda *_: (0,) * len(shape))


def _batch_spec(d, tiles_per_batch):
    return pl.BlockSpec((None, 1, d), lambda i: (i // tiles_per_batch, 0, 0))


def _inproj(x2, gain, sc, sh, wq, wr, wl, wg, seq):
    n, d = x2.shape
    tm = min(TM_PROJ, seq)
    tpb = seq // tm
    outs = [(wq.shape[1], BF16), (wr.shape[1], F32), (wl.shape[1], F32), (wg.shape[1], F32)]
    return pl.pallas_call(
        _inproj_kernel,
        grid=(n // tm,),
        in_specs=[pl.BlockSpec((tm, d), lambda i: (i, 0)),
                  _const_spec((1, d)), _batch_spec(d, tpb), _batch_spec(d, tpb),
                  _const_spec(wq.shape), _const_spec(wr.shape), _const_spec(wl.shape),
                  _const_spec(wg.shape)],
        out_specs=[pl.BlockSpec((tm, w), lambda i: (i, 0)) for w, _ in outs],
        out_shape=[jax.ShapeDtypeStruct((n, w), dt) for w, dt in outs],
        compiler_params=_params("parallel"),
        name="inproj",
    )(x2, gain, sc, sh, wq, wr, wl, wg)


def _sb_attn_kernel(q_ref, k_ref, v_ref, o_ref, acc_ref, *, scale):
    t = q_ref.shape[0]
    qi = pl.program_id(2)
    row = lax.broadcasted_iota(jnp.int32, (t, t), 0)
    col = lax.broadcasted_iota(jnp.int32, (t, t), 1)
    tri = jnp.where(row > col, 1.0, 0.0).astype(BF16)
    strict = col < row
    lane = lax.broadcasted_iota(jnp.int32, (t, LANES), 1)
    first = lane < HEAD_DIM
    q2 = q_ref[...].astype(F32) * (scale * math.log2(math.e))
    qs = (jnp.where(first, q2, 0.0).astype(BF16), jnp.where(first, 0.0, q2).astype(BF16))

    def blocks(kbs, runs, masked):
        k2 = [k_ref[pl.ds(pl.multiple_of(kb * t, t), t), :] for kb in kbs]
        v2 = [v_ref[pl.ds(pl.multiple_of(kb * t, t), t), :] for kb in kbs]
        z = [[lax.dot_general(qs[hh], k2[j], (_NT, ((), ())), preferred_element_type=F32) for hh in range(2)]
             for j in range(len(kbs))]
        runs = list(runs)
        for j in range(len(kbs)):
            for hh in range(2):
                zz = z[j][hh]
                neg_part = jnp.minimum(zz, 0.0)
                neg_other = neg_part - zz
                soft = jnp.log2(1.0 + jnp.exp2(neg_part + neg_other))
                log_beta = neg_part - soft
                log_keep = neg_other - soft
                if masked[j]:
                    log_keep = jnp.where(strict, log_keep, 0.0)
                later = jnp.dot(log_keep.astype(BF16), tri, preferred_element_type=F32)
                att = jnp.exp2(log_beta + later + runs[hh])
                if masked[j]:
                    att = jnp.where(strict, att, 0.0)
                acc_ref[hh] += jnp.dot(att.astype(BF16), v2[j], preferred_element_type=F32)
                runs[hh] = runs[hh] + jnp.sum(log_keep, axis=-1, keepdims=True)
        return tuple(runs)

    acc_ref[...] = jnp.zeros_like(acc_ref)
    zero = jnp.zeros((t, 1), F32)
    runs = lax.cond(qi > 0,
                    lambda: blocks([qi, qi - 1], (zero, zero), (True, False)),
                    lambda: blocks([qi], (zero, zero), (True,)))

    def live(state):
        kb, r = state
        return jnp.logical_and(kb >= 0, jnp.max(jnp.maximum(r[0], r[1])) > ATT_DEAD_LOG2)

    def body(state):
        kb, r = state
        return kb - 1, blocks([kb], r, (False,))

    lax.while_loop(live, body, (qi - 2, runs))
    o_ref[...] = jnp.where(first, acc_ref[0], acc_ref[1]).astype(o_ref.dtype)


def _sb_attention(qkv, batch, seq, width):
    n = qkv.shape[0]
    t = min(T_ATT, seq)
    n_pairs = width // LANES
    nq = seq // t
    kernel = functools.partial(_sb_attn_kernel, scale=1.0 / math.sqrt(HEAD_DIM))
    return pl.pallas_call(
        kernel,
        grid=(batch, n_pairs, nq),
        in_specs=[pl.BlockSpec((t, LANES), lambda b, p, i: (b * nq + i, p)),
                  pl.BlockSpec((seq, LANES), lambda b, p, i: (b, n_pairs + p)),
                  pl.BlockSpec((seq, LANES), lambda b, p, i: (b, 2 * n_pairs + p))],
        out_specs=pl.BlockSpec((t, LANES), lambda b, p, i: (b * nq + i, p)),
        out_shape=jax.ShapeDtypeStruct((n, width), BF16),
        scratch_shapes=[pltpu.VMEM((2, t, LANES), F32)],
        compiler_params=_params("parallel", "parallel", "parallel"),
        name="sb_attn",
    )(qkv, qkv, qkv)


def _shift_rows(p, halo_row):
    rolled = pltpu.roll(p, 1, 0)
    row = lax.broadcasted_iota(jnp.int32, p.shape, 0)
    return jnp.where(row == 0, halo_row, rolled)


def _rw_chunk_kernel(rkv_ref, rkv_halo_ref, lora_ref, lora_halo_ref, mu_rkv_ref, mu_lora_ref,
                     w0_ref, wd_ref, a0_ref, wa_ref, wg_ref, kk_ref, ka_ref, rk_ref, hsum_ref,
                     q_out, yv_out, bonus_out, g_out, m_out, n_out, *, tiles_per_batch, chunk):
    i = pl.program_id(0)
    keep = jnp.where(i % tiles_per_batch == 0, 0.0, 1.0)
    w = w0_ref.shape[1]
    tm = rkv_ref.shape[0]
    c = chunk
    hd = HEAD_DIM
    n_heads = w // hd

    p = rkv_ref[...]
    prev = _shift_rows(p, rkv_halo_ref[SUBLANES - 1:SUBLANES, :] * keep)
    xs = p + mu_rkv_ref[...] * (prev - p)
    pl_ = lora_ref[...]
    prevl = _shift_rows(pl_, lora_halo_ref[SUBLANES - 1:SUBLANES, :] * keep)
    xl = pl_ + mu_lora_ref[...] * (prevl - pl_)

    r = xs[:, :w]
    k = xs[:, w:2 * w]
    v = xs[:, 2 * w:]
    dec_in = w0_ref[...] + _mm1(jnp.tanh(xl), wd_ref[...])
    neg = -dec_in
    softplus = jnp.maximum(neg, 0.0) + jnp.log(1.0 + jnp.exp(-jnp.abs(neg)))
    lw = -jnp.exp(-softplus - 0.5)
    a_gate = _sigmoid(a0_ref[...] + _mm1(xl, wa_ref[...]))
    g_out[...] = _mm1(_sigmoid(xl), wg_ref[...])
    kk = k * kk_ref[...]
    k = k * (1.0 + (a_gate - 1.0) * ka_ref[...])
    hsum = hsum_ref[...]
    kkn = kk / jnp.maximum(jnp.sqrt(_mm01(kk * kk, hsum, "b")), L2_EPS)
    a = -kkn
    b = kkn * a_gate
    bonus_out[...] = _mm01(r * k * rk_ref[...], hsum, "b") * v

    row = lax.broadcasted_iota(jnp.int32, (c, c), 0)
    col = lax.broadcasted_iota(jnp.int32, (c, c), 1)
    incl = col <= row
    strict = col < row
    tril = jnp.where(incl, 1.0, 0.0).astype(BF16)
    eye = jnp.where(row == col, 1.0, 0.0)

    a_t, r_t, b_t, k_t, b_h, k_h, v_p, g_c = [], [], [], [], [], [], [], []
    for g in range(tm // c):
        rs = slice(g * c, (g + 1) * c)
        lw_c = lw[rs]
        cum = _mm01(tril, lw_c, "a")
        total = cum[c - 1:c, :]
        inv = jnp.exp(-cum)
        tail = jnp.exp(total - cum)
        full = {"a_t": a[rs] * jnp.exp(cum - lw_c), "r_t": r[rs] * jnp.exp(cum), "b_t": b[rs] * inv,
                "k_t": k[rs] * inv, "b_h": b[rs] * tail, "k_h": k[rs] * tail, "v": v[rs],
                "g_c": jnp.exp(total)}
        for h in range(n_heads):
            sl = slice(h * hd, (h + 1) * hd)
            a_t.append(full["a_t"][:, sl]); r_t.append(full["r_t"][:, sl])
            b_t.append(full["b_t"][:, sl]); k_t.append(full["k_t"][:, sl])
            b_h.append(full["b_h"][:, sl]); k_h.append(full["k_h"][:, sl])
            v_p.append(full["v"][:, sl]); g_c.append(full["g_c"][:, sl])
    n_prob = len(a_t)
    prob = range(n_prob)

    lhs = [jnp.concatenate([a_t[j], r_t[j]], axis=0) for j in prob]
    gram_b = [_mm1(lhs[j], b_t[j], _NT) for j in prob]
    gram_k = [_mm1(lhs[j], k_t[j], _NT) for j in prob]
    a_ab = [jnp.where(strict, gram_b[j][:c], 0.0) for j in prob]
    a_rb = [jnp.where(incl, gram_b[j][c:], 0.0) for j in prob]
    a_ak = [jnp.where(strict, gram_k[j][:c], 0.0) for j in prob]
    a_rk = [jnp.where(incl, gram_k[j][c:], 0.0) for j in prob]
    ak_v = [_mm1(a_ak[j], v_p[j]) for j in prob]
    rk_v = [_mm1(a_rk[j], v_p[j]) for j in prob]
    wu = [jnp.concatenate([a_t[j], ak_v[j]], axis=1) for j in prob]
    power = a_ab
    n_apply = max(1, math.ceil(math.log2(c)))
    for step in range(n_apply):
        wu = [wu[j] + _mm1(power[j], wu[j]) for j in prob]
        if step + 1 < n_apply:
            power = [_mm1(power[j], power[j]) for j in prob]
    qy = [_mm1(a_rb[j], wu[j]) + jnp.concatenate([r_t[j], rk_v[j]], axis=1) for j in prob]
    wu_b = [_mm1(wu[j], b_h[j], _TN) for j in prob]
    v_k = [_mm1(v_p[j], k_h[j], _TN) for j in prob]
    for j in prob:
        g, h = divmod(j, n_heads)
        rs = slice(g * c, (g + 1) * c)
        sl = slice(h * hd, (h + 1) * hd)
        q_out[rs, sl] = qy[j][:, :hd]
        yv_out[rs, sl] = qy[j][:, hd:]
        m_out[g, h] = wu_b[j][:hd] + eye * g_c[j]
        n_out[g, h] = wu_b[j][hd:] + v_k[j]


def _rw_chunk(rkv, lora, mu_rkv, mu_lora, w0, wd_pad, a0, wa_pad, wg_pad, k_k, k_a, r_k, hsum, seq):
    n, w3 = rkv.shape
    w = w3 // 3
    wl = lora.shape[1]
    c = min(C_SCAN, seq)
    tm = min(TM_CHUNK, seq)
    tpb = seq // tm
    cpt = tm // c
    hb = tm // SUBLANES
    n_heads = w // HEAD_DIM
    kernel = functools.partial(_rw_chunk_kernel, tiles_per_batch=tpb, chunk=c)
    row = lambda i: (i, 0)
    halo = lambda i: (jnp.maximum(i * hb - 1, 0), 0)
    vec = lambda a: a.reshape(1, -1)
    mat = pl.BlockSpec((cpt, n_heads, HEAD_DIM, HEAD_DIM), lambda i: (i, 0, 0, 0))
    mat_shape = jax.ShapeDtypeStruct((n // c, n_heads, HEAD_DIM, HEAD_DIM), F32)
    return pl.pallas_call(
        kernel,
        grid=(n // tm,),
        in_specs=[pl.BlockSpec((tm, w3), row), pl.BlockSpec((SUBLANES, w3), halo),
                  pl.BlockSpec((tm, wl), row), pl.BlockSpec((SUBLANES, wl), halo),
                  _const_spec((1, w3)), _const_spec((1, wl)),
                  _const_spec((1, w)), _const_spec((wl, w)), _const_spec((1, w)),
                  _const_spec((wl, w)), _const_spec((wl, w)),
                  _const_spec((1, w)), _const_spec((1, w)), _const_spec((1, w)),
                  _const_spec((w, w))],
        out_specs=[pl.BlockSpec((tm, w), row)] * 4 + [mat, mat],
        out_shape=[jax.ShapeDtypeStruct((n, w), F32)] * 4 + [mat_shape, mat_shape],
        compiler_params=_params("parallel"),
        name="rw_chunk",
    )(rkv, rkv, lora, lora, vec(mu_rkv), vec(mu_lora), vec(w0), wd_pad, vec(a0), wa_pad, wg_pad,
      vec(k_k), vec(k_a), vec(r_k), hsum)


def _rw_state_kernel(q_ref, yv_ref, bonus_ref, g_ref, m_ref, n_ref, lnw_ref, lnb_ref, hsum_ref,
                     o_ref, state_ref):
    n_batch, c, w = q_ref.shape
    hd = HEAD_DIM
    n_heads = w // hd

    @pl.when(pl.program_id(0) == 0)
    def _():
        state_ref[...] = jnp.zeros_like(state_ref)

    pairs = [(bi, h) for bi in range(n_batch) for h in range(n_heads)]
    state = [state_ref[bi, h] for bi, h in pairs]
    y = [_mm1(q_ref[bi, :, h * hd:(h + 1) * hd], state[j], _NT) + yv_ref[bi, :, h * hd:(h + 1) * hd]
         for j, (bi, h) in enumerate(pairs)]
    new_state = [_mm3(state[j], m_ref[bi, h]) + n_ref[bi, h] for j, (bi, h) in enumerate(pairs)]
    for j, (bi, h) in enumerate(pairs):
        state_ref[bi, h] = new_state[j]

    hsum = hsum_ref[...]
    for bi in range(n_batch):
        y_b = jnp.concatenate(y[bi * n_heads:(bi + 1) * n_heads], axis=1)
        mean = _mm01(y_b, hsum, "b") * (1.0 / hd)
        dev = y_b - mean
        var = _mm01(dev * dev, hsum, "b") * (1.0 / hd)
        yn = dev * lax.rsqrt(var + GN_EPS) * lnw_ref[...] + lnb_ref[...]
        o_ref[bi] = ((yn + bonus_ref[bi]) * g_ref[bi]).astype(o_ref.dtype)


def _rw_state(q, yv, bonus, g, m, n_mat, lnx_w, lnx_b, hsum, batch, seq):
    w = q.shape[1]
    c = min(C_SCAN, seq)
    nc = seq // c
    n_heads = w // HEAD_DIM
    rows = pl.BlockSpec((batch, c, w), lambda ci: (0, ci, 0))
    mats = pl.BlockSpec((batch, None, n_heads, HEAD_DIM, HEAD_DIM), lambda ci: (0, ci, 0, 0, 0))
    to3 = lambda a: a.reshape(batch, seq, w)
    to5 = lambda a: a.reshape(batch, nc, n_heads, HEAD_DIM, HEAD_DIM)
    out = pl.pallas_call(
        _rw_state_kernel,
        grid=(nc,),
        in_specs=[rows] * 4 + [mats, mats, _const_spec((1, w)), _const_spec((1, w)), _const_spec((w, w))],
        out_specs=rows,
        out_shape=jax.ShapeDtypeStruct((batch, seq, w), BF16),
        scratch_shapes=[pltpu.VMEM((batch, n_heads, HEAD_DIM, HEAD_DIM), F32)],
        compiler_params=_params("arbitrary"),
        name="rw_state",
    )(to3(q), to3(yv), to3(bonus), to3(g), to5(m), to5(n_mat), lnx_w.reshape(1, w), lnx_b.reshape(1, w), hsum)
    return out.reshape(batch * seq, w)


def _lane_min_index(mask, lane):
    return jnp.min(jnp.where(mask, lane, 4 * LANES), axis=-1, keepdims=True)


def _merge_kernel(x_ref, ysb_ref, yrw_ref, gl_ref, gb_ref, wsb_ref, wrw_ref, wout_ref,
                  g1_ref, gt1_ref, g2_ref, sc2_ref, sh2_ref, wrt_ref,
                  x1_ref, h2_ref, route_ref, *, n_groups, per_group):
    d = x_ref.shape[1]
    tm = x_ref.shape[0]
    gates = _sigmoid(gl_ref[...] + gb_ref[...])
    merged = (gates[:, :d] * jnp.dot(ysb_ref[...], wsb_ref[...], preferred_element_type=F32)
              + gates[:, d:] * jnp.dot(yrw_ref[...], wrw_ref[...], preferred_element_type=F32))
    y = jnp.dot(merged.astype(BF16), wout_ref[...], preferred_element_type=F32)
    x1 = x_ref[...] + gt1_ref[...] * (_rms(y) * g1_ref[...])
    x1_ref[...] = x1
    h2 = _rms(x1) * g2_ref[...] * (1.0 + sc2_ref[...]) + sh2_ref[...]
    for j in range(d // LANES):
        h2_ref[pl.ds(j, tm, stride=SUBLANES), :] = h2[:, j * LANES:(j + 1) * LANES]

    h_hi, h_lo = _split(h2)
    hi_part = jnp.dot(h_hi, wrt_ref[...], preferred_element_type=F32)
    lg = hi_part[:, :LANES] + (hi_part[:, LANES:] + jnp.dot(h_lo, wrt_ref[:, :LANES], preferred_element_type=F32))
    lane = lax.broadcasted_iota(jnp.int32, lg.shape, 1)
    neg_inf = -jnp.inf
    gmask = lane < n_groups
    gmax = jnp.max(jnp.where(gmask, lg, neg_inf), axis=-1, keepdims=True)
    gsel = _lane_min_index(gmask & (lg == gmax), lane)
    gsum = jnp.sum(jnp.where(gmask, jnp.exp(lg - gmax), 0.0), axis=-1, keepdims=True)
    group_w = 1.0 / gsum
    lo = n_groups + gsel * per_group
    emask = (lane >= lo) & (lane < lo + per_group)
    m1 = jnp.max(jnp.where(emask, lg, neg_inf), axis=-1, keepdims=True)
    i1 = _lane_min_index(emask & (lg == m1), lane)
    rest = emask & (lane != i1)
    m2 = jnp.max(jnp.where(rest, lg, neg_inf), axis=-1, keepdims=True)
    i2 = _lane_min_index(rest & (lg == m2), lane)
    psum = jnp.sum(jnp.where(emask, jnp.exp(lg - m1), 0.0), axis=-1, keepdims=True)
    p1 = 1.0 / psum
    p2 = jnp.exp(m2 - m1) / psum
    w1 = group_w * (p1 / (p1 + p2))
    w2 = group_w * (p2 / (p1 + p2))
    e1 = (i1 - n_groups).astype(F32)
    e2 = (i2 - n_groups).astype(F32)
    route_ref[...] = jnp.where(lane == 0, e1, jnp.where(lane == 1, e2,
                               jnp.where(lane == 2, w1, jnp.where(lane == 3, w2, 0.0))))


def _merge(x2, ysb, yrw, gate_logits, gate_bias, wsb, wrw, wout, g1, gt1, g2, sc2, sh2, wrt,
           seq, n_groups, per_group):
    n, d = x2.shape
    tm = min(TM_MERGE, seq)
    tpb = seq // tm
    w = ysb.shape[1]
    row = lambda i: (i, 0)
    kernel = functools.partial(_merge_kernel, n_groups=n_groups, per_group=per_group)
    return pl.pallas_call(
        kernel,
        grid=(n // tm,),
        in_specs=[pl.BlockSpec((tm, d), row), pl.BlockSpec((tm, w), row), pl.BlockSpec((tm, w), row),
                  pl.BlockSpec((tm, 2 * d), row), _const_spec((1, 2 * d)),
                  _const_spec(wsb.shape), _const_spec(wrw.shape), _const_spec(wout.shape),
                  _const_spec((1, d)), _batch_spec(d, tpb), _const_spec((1, d)),
                  _batch_spec(d, tpb), _batch_spec(d, tpb), _const_spec(wrt.shape)],
        out_specs=[pl.BlockSpec((tm, d), row),
                   pl.BlockSpec((tm * SUBLANES, LANES), row),
                   pl.BlockSpec((tm, LANES), row)],
        out_shape=[jax.ShapeDtypeStruct((n, d), F32),
                   jax.ShapeDtypeStruct((n * SUBLANES, LANES), F32),
                   jax.ShapeDtypeStruct((n, LANES), F32)],
        compiler_params=_params("parallel"),
        name="merge",
    )(x2, ysb, yrw, gate_logits, gate_bias.reshape(1, -1), wsb, wrw, wout, g1, gt1, g2, sc2, sh2, wrt)


def _tile_rows(ref, index):
    return ref.at[pl.ds(pl.multiple_of(index * SUBLANES, SUBLANES), SUBLANES), :]


def _row_gather(src_hbm, idx_ref, base, dst, sem, n_rows):
    def body(r8, _):
        for u in range(GATHER_UNROLL):
            r = r8 * GATHER_UNROLL + u
            pltpu.make_async_copy(_tile_rows(src_hbm, idx_ref[base + r]), _tile_rows(dst, r), sem).start()
        return 0
    lax.fori_loop(0, n_rows // GATHER_UNROLL, body, 0)


def _row_gather_wait(src_hbm, dst, sem):
    pltpu.make_async_copy(src_hbm.at[pl.ds(0, dst.shape[0]), :], dst, sem).wait()


def _expert_kernel(tile_expert_ref, tile_valid_ref, row_tok_ref,
                   h2_hbm, w1_ref, w3_ref, wd_ref,
                   out_ref, gbuf, sems, w1_bf, w3_bf, wd_bf):
    tm = out_ref.shape[0] // SUBLANES
    d = w1_ref.shape[0]
    j = pl.program_id(0)
    nt = pl.num_programs(0)
    slot = j % 2

    @pl.when(jnp.logical_or(j == 0, tile_expert_ref[j] != tile_expert_ref[jnp.maximum(j - 1, 0)]))
    def _():
        w1_bf[...] = w1_ref[...].astype(BF16)
        w3_bf[...] = w3_ref[...].astype(BF16)
        wd_bf[...] = wd_ref[...].astype(BF16)

    def start(tile, s):
        @pl.when(tile_valid_ref[tile] == 1)
        def _():
            _row_gather(h2_hbm, row_tok_ref, tile * tm, gbuf.at[s], sems.at[s], tm)

    @pl.when(j == 0)
    def _():
        start(0, 0)

    @pl.when(j + 1 < nt)
    def _():
        start(j + 1, 1 - slot)

    @pl.when(tile_valid_ref[j] == 1)
    def _():
        _row_gather_wait(h2_hbm, gbuf.at[slot], sems.at[slot])
        cur = gbuf.at[slot]
        x = jnp.concatenate([cur[pl.ds(c, tm, stride=SUBLANES), :].astype(BF16)
                             for c in range(d // LANES)], axis=1)
        up1 = jnp.dot(x, w1_bf[...], preferred_element_type=F32)
        up3 = jnp.dot(x, w3_bf[...], preferred_element_type=F32)
        hid = up1 * _sigmoid(up1) * up3
        y = jnp.dot(hid.astype(BF16), wd_bf[...], preferred_element_type=F32)
        for c in range(d // LANES):
            out_ref[pl.ds(c, tm, stride=SUBLANES), :] = y[:, c * LANES:(c + 1) * LANES]

    @pl.when(tile_valid_ref[j] == 0)
    def _():
        out_ref[...] = jnp.zeros_like(out_ref)


def _experts(h2_tiles, tile_expert, tile_valid, row_tok, w1, w3, wd):
    n_tiles = tile_expert.shape[0]
    tm = TM_EXP
    _, d, de = w1.shape
    rows = n_tiles * tm
    grid_spec = pltpu.PrefetchScalarGridSpec(
        num_scalar_prefetch=3,
        grid=(n_tiles,),
        in_specs=[pl.BlockSpec(memory_space=pl.ANY),
                  pl.BlockSpec((None, d, de), lambda j, te, tv, rt: (te[j], 0, 0)),
                  pl.BlockSpec((None, d, de), lambda j, te, tv, rt: (te[j], 0, 0)),
                  pl.BlockSpec((None, de, d), lambda j, te, tv, rt: (te[j], 0, 0))],
        out_specs=pl.BlockSpec((tm * SUBLANES, LANES), lambda j, te, tv, rt: (j, 0)),
        scratch_shapes=[pltpu.VMEM((2, tm * SUBLANES, LANES), F32),
                        pltpu.SemaphoreType.DMA((2,)),
                        pltpu.VMEM((d, de), BF16), pltpu.VMEM((d, de), BF16), pltpu.VMEM((de, d), BF16)],
    )
    return pl.pallas_call(
        _expert_kernel,
        grid_spec=grid_spec,
        out_shape=jax.ShapeDtypeStruct((rows * SUBLANES, LANES), F32),
        compiler_params=_params("arbitrary"),
        name="experts",
    )(tile_expert, tile_valid, row_tok, h2_tiles, w1, w3, wd)


def _combine_kernel(pos_ref,
                    ye_hbm, x1_ref, route_ref, g3_ref, gt2_ref,
                    out_ref, gbuf, sems, *, n_tokens):
    tm, d = x1_ref.shape
    i = pl.program_id(0)
    nt = pl.num_programs(0)
    slot = i % 2

    def start(tile, s):
        for k in range(TOP_K_IN_GROUP):
            _row_gather(ye_hbm, pos_ref, k * n_tokens + tile * tm, gbuf.at[s, k], sems.at[s], tm)

    @pl.when(i == 0)
    def _():
        start(0, 0)

    @pl.when(i + 1 < nt)
    def _():
        start(i + 1, 1 - slot)

    for k in range(TOP_K_IN_GROUP):
        _row_gather_wait(ye_hbm, gbuf.at[slot, k], sems.at[slot])
    wts = [route_ref[:, TOP_K_IN_GROUP + k:TOP_K_IN_GROUP + k + 1] for k in range(TOP_K_IN_GROUP)]
    y = jnp.concatenate(
        [sum(wts[k] * gbuf[slot, k, pl.ds(c, tm, stride=SUBLANES), :] for k in range(TOP_K_IN_GROUP))
         for c in range(d // LANES)], axis=1)
    out_ref[...] = x1_ref[...] + gt2_ref[...] * (_rms(y) * g3_ref[...])


def _combine(ye_tiles, pos, x1, route, g3, gt2, seq):
    n, d = x1.shape
    tm = min(TM_COMB, seq)
    tpb = seq // tm
    kernel = functools.partial(_combine_kernel, n_tokens=n)
    grid_spec = pltpu.PrefetchScalarGridSpec(
        num_scalar_prefetch=1,
        grid=(n // tm,),
        in_specs=[pl.BlockSpec(memory_space=pl.ANY),
                  pl.BlockSpec((tm, d), lambda i, p: (i, 0)),
                  pl.BlockSpec((tm, LANES), lambda i, p: (i, 0)),
                  pl.BlockSpec((1, d), lambda i, p: (0, 0)),
                  pl.BlockSpec((None, 1, d), lambda i, p: (i // tpb, 0, 0))],
        out_specs=pl.BlockSpec((tm, d), lambda i, p: (i, 0)),
        scratch_shapes=[pltpu.VMEM((2, TOP_K_IN_GROUP, tm * SUBLANES, LANES), F32),
                        pltpu.SemaphoreType.DMA((2,))],
    )
    return pl.pallas_call(
        kernel,
        grid_spec=grid_spec,
        out_shape=jax.ShapeDtypeStruct((n, d), F32),
        compiler_params=_params("arbitrary"),
        name="combine",
    )(pos, ye_tiles, x1, route, g3, gt2)


def _plan_rows(route, n_experts, tm):
    n = route.shape[0]
    pairs = TOP_K_IN_GROUP * n
    expert = jnp.concatenate([route[:, k] for k in range(TOP_K_IN_GROUP)]).astype(jnp.int32)
    token = jnp.tile(jnp.arange(n, dtype=jnp.int32), TOP_K_IN_GROUP)
    onehot = (expert[:, None] == jnp.arange(n_experts, dtype=jnp.int32)[None, :]).astype(jnp.int32)
    rank = jnp.sum(jnp.cumsum(onehot, axis=0) * onehot, axis=1) - 1
    counts = jnp.sum(onehot, axis=0)
    padded = -(-counts // tm) * tm
    ends = jnp.cumsum(padded)
    offsets = ends - padded
    pos = jnp.sum(onehot * offsets[None, :], axis=1) + rank
    n_tiles = pairs // tm + n_experts
    rows = n_tiles * tm
    row_tok = jnp.zeros((rows,), jnp.int32).at[pos].set(token)
    tile_start = jnp.arange(n_tiles, dtype=jnp.int32) * tm
    tile_expert = jnp.minimum(jnp.sum((ends[None, :] <= tile_start[:, None]).astype(jnp.int32), axis=1),
                              n_experts - 1)
    tile_valid = (tile_start < ends[-1]).astype(jnp.int32)
    return tile_expert, tile_valid, row_tok, pos.astype(jnp.int32)


def _layer(x, c, w_mod, b_mod, norm_gains, w_in, gate_bias, shift_mu, w0, w_decay_up, a0, w_iclr_up,
           w_gate_up, k_k, k_a, r_k, lnx_w, lnx_b, w_o_sb, w_o_rw, w_out, w_group, w_router,
           w_up1, w_up3, w_down):
    batch, seq, d = x.shape
    n = batch * seq
    sb_width = w_o_sb.shape[0]
    rw_width = w_o_rw.shape[0]
    decay_lora, iclr_lora, gate_lora = w_decay_up.shape[0], w_iclr_up.shape[0], w_gate_up.shape[0]
    lora = decay_lora + iclr_lora + gate_lora
    lora_pad = -(-lora // LANES) * LANES
    n_groups = w_group.shape[1]
    n_experts = w_router.shape[1]
    per_group = n_experts // n_groups

    mod = _modulation(c, w_mod, b_mod)
    sh1, sc1, gt1, sh2, sc2, gt2 = [m.reshape(batch, 1, d) for m in jnp.split(mod, 6, axis=-1)]
    gains = norm_gains.reshape(4, 1, d)

    o1 = 3 * sb_width
    o2 = o1 + 3 * rw_width
    o3 = o2 + lora
    wq = w_in[:, :o1].astype(BF16)
    wr = w_in[:, o1:o2].astype(BF16)
    wl = jnp.zeros((d, lora_pad), BF16).at[:, :lora].set(w_in[:, o2:o3].astype(BF16))
    wg = w_in[:, o3:].astype(BF16)
    x2 = x.reshape(n, d)
    qkv, rkv, lora_cols, gate_logits = _inproj(x2, gains[0], sc1, sh1, wq, wr, wl, wg, seq)

    y_sb = _sb_attention(qkv, batch, seq, sb_width)

    mu_rkv = shift_mu[:3 * rw_width]
    mu_lora = jnp.zeros((lora_pad,), F32).at[:lora].set(shift_mu[3 * rw_width:])
    l1 = decay_lora
    l2 = l1 + iclr_lora
    wd_pad = jnp.zeros((lora_pad, rw_width), F32).at[:l1].set(w_decay_up)
    wa_pad = jnp.zeros((lora_pad, rw_width), F32).at[l1:l2].set(w_iclr_up)
    wg_pad = jnp.zeros((lora_pad, rw_width), F32).at[l2:lora].set(w_gate_up)
    head = jnp.arange(rw_width) // HEAD_DIM
    hsum = (head[:, None] == head[None, :]).astype(BF16)
    q_c, yv_c, bonus, g, m_c, n_c = _rw_chunk(rkv, lora_cols, mu_rkv, mu_lora, w0, wd_pad, a0, wa_pad,
                                              wg_pad, k_k, k_a, r_k.reshape(-1), hsum, seq)
    y_rw = _rw_state(q_c, yv_c, bonus, g, m_c, n_c, lnx_w, lnx_b, hsum, batch, seq)

    wrt = jnp.zeros((d, LANES), F32).at[:, :n_groups].set(w_group)
    wrt = wrt.at[:, n_groups:n_groups + n_experts].set(w_router)
    wrt = jnp.concatenate(_split(wrt), axis=1)
    x1, h2_tiles, route = _merge(x2, y_sb, y_rw, gate_logits, gate_bias, w_o_sb.astype(BF16),
                                 w_o_rw.astype(BF16), w_out.astype(BF16), gains[1], gt1, gains[2],
                                 sc2, sh2, wrt, seq, n_groups, per_group)

    tile_expert, tile_valid, row_tok, pos = _plan_rows(route, n_experts, TM_EXP)
    ye = _experts(h2_tiles, tile_expert, tile_valid, row_tok,
                  w_up1, w_up3, w_down)
    out = _combine(ye, pos, x1, route, gains[3], gt2, seq)
    return out.reshape(batch, seq, d)


def kernel(x, c, w_mod, b_mod, norm_gains, w_in, gate_bias, shift_mu, w0, w_decay_up, a0, w_iclr_up,
           w_gate_up, k_k, k_a, r_k, lnx_w, lnx_b, w_o_sb, w_o_rw, w_out, w_group, w_router,
           w_up1, w_up3, w_down):
    for l in range(w_mod.shape[0]):
        x = _layer(x, c, w_mod[l], b_mod[l], norm_gains[l], w_in[l], gate_bias[l], shift_mu[l], w0[l],
                   w_decay_up[l], a0[l], w_iclr_up[l], w_gate_up[l], k_k[l], k_a[l], r_k[l], lnx_w[l],
                   lnx_b[l], w_o_sb[l], w_o_rw[l], w_out[l], w_group[l], w_router[l], w_up1[l],
                   w_up3[l], w_down[l])
    return x
```

```python
import functools
import math

import jax
import jax.numpy as jnp
from jax import lax
from jax.experimental import pallas as pl
from jax.experimental.pallas import tpu as pltpu

F32 = jnp.float32
BF16 = jnp.bfloat16
HIGHEST = lax.Precision.HIGHEST

HEAD_DIM = 64
TOP_K_IN_GROUP = 2
NORM_EPS = 1e-6
GN_EPS = 64e-5
L2_EPS = 1e-12

LANES = 128
SUBLANES = 8
VMEM_LIMIT_BYTES = 56 * 1024 * 1024

TM_PROJ = 512
T_ATT = 256
ATT_DEAD_LOG2 = -150.0
C_SCAN = 64
TM_CHUNK = 256
TM_MERGE = 512
TM_EXP = 256
TM_COMB = 256
GATHER_UNROLL = 8

_NN = ((1,), (0,))
_NT = ((1,), (1,))
_TN = ((0,), (0,))


def _params(*sem):
    return pltpu.CompilerParams(dimension_semantics=sem, vmem_limit_bytes=VMEM_LIMIT_BYTES)


def _hdot(a, b):
    return jnp.dot(a, b, precision=HIGHEST, preferred_element_type=F32)


def _split(x):
    hi = x.astype(BF16)
    return hi, (x - hi.astype(F32)).astype(BF16)


def _mm1(a, b, dims=_NN):
    return lax.dot_general(a.astype(BF16), b.astype(BF16), (dims, ((), ())), preferred_element_type=F32)


def _mm01(a, b, exact):
    if exact == "a":
        hi, lo = _split(b)
        return jnp.dot(a, hi, preferred_element_type=F32) + jnp.dot(a, lo, preferred_element_type=F32)
    hi, lo = _split(a)
    return jnp.dot(hi, b, preferred_element_type=F32) + jnp.dot(lo, b, preferred_element_type=F32)


def _rms(x):
    return x * lax.rsqrt(jnp.mean(x * x, axis=-1, keepdims=True) + NORM_EPS)


def _sigmoid(x):
    return 1.0 / (1.0 + jnp.exp(-x))


def _mod_kernel(c_ref, w_ref, b_ref, o_ref):
    c = c_ref[...]
    o_ref[...] = _hdot(c * _sigmoid(c), w_ref[...]) + b_ref[...]


def _modulation(c, w_mod, b_mod):
    b, d = c.shape
    n_out = w_mod.shape[1]
    rows = -(-b // SUBLANES) * SUBLANES
    c_pad = jnp.zeros((rows, d), F32).at[:b].set(c)
    out = pl.pallas_call(
        _mod_kernel,
        grid=(n_out // d,),
        in_specs=[pl.BlockSpec((rows, d), lambda j: (0, 0)),
                  pl.BlockSpec((d, d), lambda j: (0, j)),
                  pl.BlockSpec((1, d), lambda j: (0, j))],
        out_specs=pl.BlockSpec((rows, d), lambda j: (0, j)),
        out_shape=jax.ShapeDtypeStruct((rows, n_out), F32),
        compiler_params=_params("parallel"),
        name="mod",
    )(c_pad, w_mod, b_mod.reshape(1, n_out))
    return out[:b]


def _inproj_kernel(x_ref, g_ref, sc_ref, sh_ref, wq_ref, wr_ref, wl_ref, wg_ref,
                   qkv_ref, rkv_ref, lora_ref, gate_ref):
    h = _rms(x_ref[...]) * g_ref[...] * (1.0 + sc_ref[...]) + sh_ref[...]
    hb = h.astype(BF16)
    qkv_ref[...] = jnp.dot(hb, wq_ref[...], preferred_element_type=F32).astype(qkv_ref.dtype)
    rkv_ref[...] = jnp.dot(hb, wr_ref[...], preferred_element_type=F32)
    lora_ref[...] = jnp.dot(hb, wl_ref[...], preferred_element_type=F32)
    gate_ref[...] = jnp.dot(hb, wg_ref[...], preferred_element_type=F32)


def _const_spec(shape):
    return pl.BlockSpec(shape, lambda *_: (0,) * len(shape))


def _batch_spec(d, tiles_per_batch):
    return pl.BlockSpec((None, 1, d), lambda i: (i // tiles_per_batch, 0, 0))


def _inproj(x2, gain, sc, sh, wq, wr, wl, wg, seq):
    n, d = x2.shape
    tm = min(TM_PROJ, seq)
    tpb = seq // tm
    outs = [(wq.shape[1], BF16), (wr.shape[1], F32), (wl.shape[1], F32), (wg.shape[1], F32)]
    return pl.pallas_call(
        _inproj_kernel,
        grid=(n // tm,),
        in_specs=[pl.BlockSpec((tm, d), lambda i: (i, 0)),
                  _const_spec((1, d)), _batch_spec(d, tpb), _batch_spec(d, tpb),
                  _const_spec(wq.shape), _const_spec(wr.shape), _const_spec(wl.shape),
                  _const_spec(wg.shape)],
        out_specs=[pl.BlockSpec((tm, w), lambda i: (i, 0)) for w, _ in outs],
        out_shape=[jax.ShapeDtypeStruct((n, w), dt) for w, dt in outs],
        compiler_params=_params("parallel"),
        name="inproj",
    )(x2, gain, sc, sh, wq, wr, wl, wg)


def _sb_attn_kernel(q_ref, k_ref, v_ref, o_ref, acc_ref, *, scale):
    t = q_ref.shape[0]
    qi = pl.program_id(2)
    row = lax.broadcasted_iota(jnp.int32, (t, t), 0)
    col = lax.broadcasted_iota(jnp.int32, (t, t), 1)
    tri = jnp.where(row > col, 1.0, 0.0).astype(BF16)
    strict = col < row
    lane = lax.broadcasted_iota(jnp.int32, (t, LANES), 1)
    first = lane < HEAD_DIM
    q2 = q_ref[...].astype(F32) * (scale * math.log2(math.e))
    qs = (jnp.where(first, q2, 0.0).astype(BF16), jnp.where(first, 0.0, q2).astype(BF16))

    def blocks(kbs, runs, masked):
        k2 = [k_ref[pl.ds(pl.multiple_of(kb * t, t), t), :] for kb in kbs]
        v2 = [v_ref[pl.ds(pl.multiple_of(kb * t, t), t), :] for kb in kbs]
        z = [[lax.dot_general(qs[hh], k2[j], (_NT, ((), ())), preferred_element_type=F32) for hh in range(2)]
             for j in range(len(kbs))]
        runs = list(runs)
        for j in range(len(kbs)):
            for hh in range(2):
                zz = z[j][hh]
                neg_part = jnp.minimum(zz, 0.0)
                neg_other = neg_part - zz
                soft = jnp.log2(1.0 + jnp.exp2(neg_part + neg_other))
                log_beta = neg_part - soft
                log_keep = neg_other - soft
                if masked[j]:
                    log_keep = jnp.where(strict, log_keep, 0.0)
                later = jnp.dot(log_keep.astype(BF16), tri, preferred_element_type=F32)
                att = jnp.exp2(log_beta + later + runs[hh])
                if masked[j]:
                    att = jnp.where(strict, att, 0.0)
                acc_ref[hh] += jnp.dot(att.astype(BF16), v2[j], preferred_element_type=F32)
                runs[hh] = runs[hh] + jnp.sum(log_keep, axis=-1, keepdims=True)
        return tuple(runs)

    acc_ref[...] = jnp.zeros_like(acc_ref)
    zero = jnp.zeros((t, 1), F32)
    runs = lax.cond(qi > 0,
                    lambda: blocks([qi, qi - 1], (zero, zero), (True, False)),
                    lambda: blocks([qi], (zero, zero), (True,)))

    def live(state):
        kb, r = state
        return jnp.logical_and(kb >= 0, jnp.max(jnp.maximum(r[0], r[1])) > ATT_DEAD_LOG2)

    def body(state):
        kb, r = state
        return kb - 1, blocks([kb], r, (False,))

    lax.while_loop(live, body, (qi - 2, runs))
    o_ref[...] = jnp.where(first, acc_ref[0], acc_ref[1]).astype(o_ref.dtype)


def _sb_attention(qkv, batch, seq, width):
    n = qkv.shape[0]
    t = min(T_ATT, seq)
    n_pairs = width // LANES
    nq = seq // t
    kernel = functools.partial(_sb_attn_kernel, scale=1.0 / math.sqrt(HEAD_DIM))
    return pl.pallas_call(
        kernel,
        grid=(batch, n_pairs, nq),
        in_specs=[pl.BlockSpec((t, LANES), lambda b, p, i: (b * nq + i, p)),
                  pl.BlockSpec((seq, LANES), lambda b, p, i: (b, n_pairs + p)),
                  pl.BlockSpec((seq, LANES), lambda b, p, i: (b, 2 * n_pairs + p))],
        out_specs=pl.BlockSpec((t, LANES), lambda b, p, i: (b * nq + i, p)),
        out_shape=jax.ShapeDtypeStruct((n, width), BF16),
        scratch_shapes=[pltpu.VMEM((2, t, LANES), F32)],
        compiler_params=_params("parallel", "parallel", "parallel"),
        name="sb_attn",
    )(qkv, qkv, qkv)


def _shift_rows(p, halo_row):
    rolled = pltpu.roll(p, 1, 0)
    row = lax.broadcasted_iota(jnp.int32, p.shape, 0)
    return jnp.where(row == 0, halo_row, rolled)


def _rw_chunk_kernel(rkv_ref, rkv_halo_ref, lora_ref, lora_halo_ref, mu_rkv_ref, mu_lora_ref,
                     w0_ref, wd_ref, a0_ref, wa_ref, wg_ref, kk_ref, ka_ref, rk_ref, hsum_ref,
                     q_out, yv_out, bonus_out, g_out, m_out, n_out, *, tiles_per_batch, chunk):
    i = pl.program_id(0)
    keep = jnp.where(i % tiles_per_batch == 0, 0.0, 1.0)
    w = w0_ref.shape[1]
    tm = rkv_ref.shape[0]
    c = chunk
    hd = HEAD_DIM
    n_heads = w // hd

    p = rkv_ref[...]
    prev = _shift_rows(p, rkv_halo_ref[SUBLANES - 1:SUBLANES, :] * keep)
    xs = p + mu_rkv_ref[...] * (prev - p)
    pl_ = lora_ref[...]
    prevl = _shift_rows(pl_, lora_halo_ref[SUBLANES - 1:SUBLANES, :] * keep)
    xl = pl_ + mu_lora_ref[...] * (prevl - pl_)

    r = xs[:, :w]
    k = xs[:, w:2 * w]
    v = xs[:, 2 * w:]
    dec_in = w0_ref[...] + _mm1(jnp.tanh(xl), wd_ref[...])
    neg = -dec_in
    softplus = jnp.maximum(neg, 0.0) + jnp.log(1.0 + jnp.exp(-jnp.abs(neg)))
    lw = -jnp.exp(-softplus - 0.5)
    a_gate = _sigmoid(a0_ref[...] + _mm1(xl, wa_ref[...]))
    g_out[...] = _mm1(_sigmoid(xl), wg_ref[...])
    kk = k * kk_ref[...]
    k = k * (1.0 + (a_gate - 1.0) * ka_ref[...])
    hsum = hsum_ref[...]
    kkn = kk / jnp.maximum(jnp.sqrt(_mm01(kk * kk, hsum, "b")), L2_EPS)
    a = -kkn
    b = kkn * a_gate
    bonus_out[...] = _mm01(r * k * rk_ref[...], hsum, "b") * v

    row = lax.broadcasted_iota(jnp.int32, (c, c), 0)
    col = lax.broadcasted_iota(jnp.int32, (c, c), 1)
    incl = col <= row
    strict = col < row
    tril = jnp.where(incl, 1.0, 0.0).astype(BF16)
    eye = jnp.where(row == col, 1.0, 0.0)

    a_t, r_t, b_t, k_t, b_h, k_h, v_p, g_c = [], [], [], [], [], [], [], []
    for g in range(tm // c):
        rs = slice(g * c, (g + 1) * c)
        lw_c = lw[rs]
        cum = _mm01(tril, lw_c, "a")
        total = cum[c - 1:c, :]
        inv = jnp.exp(-cum)
        tail = jnp.exp(total - cum)
        full = {"a_t": a[rs] * jnp.exp(cum - lw_c), "r_t": r[rs] * jnp.exp(cum), "b_t": b[rs] * inv,
                "k_t": k[rs] * inv, "b_h": b[rs] * tail, "k_h": k[rs] * tail, "v": v[rs],
                "g_c": jnp.exp(total)}
        for h in range(n_heads):
            sl = slice(h * hd, (h + 1) * hd)
            a_t.append(full["a_t"][:, sl]); r_t.append(full["r_t"][:, sl])
            b_t.append(full["b_t"][:, sl]); k_t.append(full["k_t"][:, sl])
            b_h.append(full["b_h"][:, sl]); k_h.append(full["k_h"][:, sl])
            v_p.append(full["v"][:, sl]); g_c.append(full["g_c"][:, sl])
    n_prob = len(a_t)
    prob = range(n_prob)

    lhs = [jnp.concatenate([a_t[j], r_t[j]], axis=0) for j in prob]
    gram_b = [_mm1(lhs[j], b_t[j], _NT) for j in prob]
    gram_k = [_mm1(lhs[j], k_t[j], _NT) for j in prob]
    a_ab = [jnp.where(strict, gram_b[j][:c], 0.0) for j in prob]
    a_rb = [jnp.where(incl, gram_b[j][c:], 0.0) for j in prob]
    a_ak = [jnp.where(strict, gram_k[j][:c], 0.0) for j in prob]
    a_rk = [jnp.where(incl, gram_k[j][c:], 0.0) for j in prob]
    ak_v = [_mm1(a_ak[j], v_p[j]) for j in prob]
    rk_v = [_mm1(a_rk[j], v_p[j]) for j in prob]
    wu = [jnp.concatenate([a_t[j], ak_v[j]], axis=1) for j in prob]
    power = a_ab
    n_apply = max(1, math.ceil(math.log2(c)))
    for step in range(n_apply):
        wu = [wu[j] + _mm1(power[j], wu[j]) for j in prob]
        if step + 1 < n_apply:
            power = [_mm1(power[j], power[j]) for j in prob]
    qy = [_mm1(a_rb[j], wu[j]) + jnp.concatenate([r_t[j], rk_v[j]], axis=1) for j in prob]
    wu_b = [_mm1(wu[j], b_h[j], _TN) for j in prob]
    v_k = [_mm1(v_p[j], k_h[j], _TN) for j in prob]
    for j in prob:
        g, h = divmod(j, n_heads)
        rs = slice(g * c, (g + 1) * c)
        sl = slice(h * hd, (h + 1) * hd)
        q_out[rs, sl] = qy[j][:, :hd]
        yv_out[rs, sl] = qy[j][:, hd:]
        m_out[g, h] = wu_b[j][:hd] + eye * g_c[j]
        n_out[g, h] = wu_b[j][hd:] + v_k[j]


def _rw_chunk(rkv, lora, mu_rkv, mu_lora, w0, wd_pad, a0, wa_pad, wg_pad, k_k, k_a, r_k, hsum, seq):
    n, w3 = rkv.shape
    w = w3 // 3
    wl = lora.shape[1]
    c = min(C_SCAN, seq)
    tm = min(TM_CHUNK, seq)
    tpb = seq // tm
    cpt = tm // c
    hb = tm // SUBLANES
    n_heads = w // HEAD_DIM
    kernel = functools.partial(_rw_chunk_kernel, tiles_per_batch=tpb, chunk=c)
    row = lambda i: (i, 0)
    halo = lambda i: (jnp.maximum(i * hb - 1, 0), 0)
    vec = lambda a: a.reshape(1, -1)
    mat = pl.BlockSpec((cpt, n_heads, HEAD_DIM, HEAD_DIM), lambda i: (i, 0, 0, 0))
    mat_shape = jax.ShapeDtypeStruct((n // c, n_heads, HEAD_DIM, HEAD_DIM), F32)
    return pl.pallas_call(
        kernel,
        grid=(n // tm,),
        in_specs=[pl.BlockSpec((tm, w3), row), pl.BlockSpec((SUBLANES, w3), halo),
                  pl.BlockSpec((tm, wl), row), pl.BlockSpec((SUBLANES, wl), halo),
                  _const_spec((1, w3)), _const_spec((1, wl)),
                  _const_spec((1, w)), _const_spec((wl, w)), _const_spec((1, w)),
                  _const_spec((wl, w)), _const_spec((wl, w)),
                  _const_spec((1, w)), _const_spec((1, w)), _const_spec((1, w)),
                  _const_spec((w, w))],
        out_specs=[pl.BlockSpec((tm, w), row)] * 4 + [mat, mat],
        out_shape=[jax.ShapeDtypeStruct((n, w), F32)] * 4 + [mat_shape, mat_shape],
        compiler_params=_params("parallel"),
        name="rw_chunk",
    )(rkv, rkv, lora, lora, vec(mu_rkv), vec(mu_lora), vec(w0), wd_pad, vec(a0), wa_pad, wg_pad,
      vec(k_k), vec(k_a), vec(r_k), hsum)


def _rw_state_kernel(q_ref, yv_ref, bonus_ref, g_ref, m_ref, n_ref, lnw_ref, lnb_ref, hsum_ref,
                     o_ref, state_ref):
    n_batch, c, w = q_ref.shape
    hd = HEAD_DIM
    n_heads = w // hd

    @pl.when(pl.program_id(0) == 0)
    def _():
        state_ref[...] = jnp.zeros_like(state_ref)

    pairs = [(bi, h) for bi in range(n_batch) for h in range(n_heads)]
    state = [state_ref[bi, h] for bi, h in pairs]
    y = [_mm1(q_ref[bi, :, h * hd:(h + 1) * hd], state[j], _NT) + yv_ref[bi, :, h * hd:(h + 1) * hd]
         for j, (bi, h) in enumerate(pairs)]
    new_state = [_mm1(state[j], m_ref[bi, h]) + n_ref[bi, h] for j, (bi, h) in enumerate(pairs)]
    for j, (bi, h) in enumerate(pairs):
        state_ref[bi, h] = new_state[j]

    hsum = hsum_ref[...]
    for bi in range(n_batch):
        y_b = jnp.concatenate(y[bi * n_heads:(bi + 1) * n_heads], axis=1)
        mean = _mm01(y_b, hsum, "b") * (1.0 / hd)
        dev = y_b - mean
        var = _mm01(dev * dev, hsum, "b") * (1.0 / hd)
        yn = dev * lax.rsqrt(var + GN_EPS) * lnw_ref[...] + lnb_ref[...]
        o_ref[bi] = ((yn + bonus_ref[bi]) * g_ref[bi]).astype(o_ref.dtype)


def _rw_state(q, yv, bonus, g, m, n_mat, lnx_w, lnx_b, hsum, batch, seq):
    w = q.shape[1]
    c = min(C_SCAN, seq)
    nc = seq // c
    n_heads = w // HEAD_DIM
    rows = pl.BlockSpec((batch, c, w), lambda ci: (0, ci, 0))
    mats = pl.BlockSpec((batch, None, n_heads, HEAD_DIM, HEAD_DIM), lambda ci: (0, ci, 0, 0, 0))
    to3 = lambda a: a.reshape(batch, seq, w)
    to5 = lambda a: a.reshape(batch, nc, n_heads, HEAD_DIM, HEAD_DIM)
    out = pl.pallas_call(
        _rw_state_kernel,
        grid=(nc,),
        in_specs=[rows] * 4 + [mats, mats, _const_spec((1, w)), _const_spec((1, w)), _const_spec((w, w))],
        out_specs=rows,
        out_shape=jax.ShapeDtypeStruct((batch, seq, w), BF16),
        scratch_shapes=[pltpu.VMEM((batch, n_heads, HEAD_DIM, HEAD_DIM), F32)],
        compiler_params=_params("arbitrary"),
        name="rw_state",
    )(to3(q), to3(yv), to3(bonus), to3(g), to5(m), to5(n_mat), lnx_w.reshape(1, w), lnx_b.reshape(1, w), hsum)
    return out.reshape(batch * seq, w)


def _lane_min_index(mask, lane):
    return jnp.min(jnp.where(mask, lane, 4 * LANES), axis=-1, keepdims=True)


def _merge_kernel(x_ref, ysb_ref, yrw_ref, gl_ref, gb_ref, wsb_ref, wrw_ref, wout_ref,
                  g1_ref, gt1_ref, g2_ref, sc2_ref, sh2_ref, wrt_ref,
                  x1_ref, h2_ref, route_ref, *, n_groups, per_group):
    d = x_ref.shape[1]
    tm = x_ref.shape[0]
    gates = _sigmoid(gl_ref[...] + gb_ref[...])
    merged = (gates[:, :d] * jnp.dot(ysb_ref[...], wsb_ref[...], preferred_element_type=F32)
              + gates[:, d:] * jnp.dot(yrw_ref[...], wrw_ref[...], preferred_element_type=F32))
    y = jnp.dot(merged.astype(BF16), wout_ref[...], preferred_element_type=F32)
    x1 = x_ref[...] + gt1_ref[...] * (_rms(y) * g1_ref[...])
    x1_ref[...] = x1
    h2 = _rms(x1) * g2_ref[...] * (1.0 + sc2_ref[...]) + sh2_ref[...]
    for j in range(d // LANES):
        h2_ref[pl.ds(j, tm, stride=SUBLANES), :] = h2[:, j * LANES:(j + 1) * LANES]

    h_hi, h_lo = _split(h2)
    hi_part = jnp.dot(h_hi, wrt_ref[...], preferred_element_type=F32)
    lg = hi_part[:, :LANES] + (hi_part[:, LANES:] + jnp.dot(h_lo, wrt_ref[:, :LANES], preferred_element_type=F32))
    lane = lax.broadcasted_iota(jnp.int32, lg.shape, 1)
    neg_inf = -jnp.inf
    gmask = lane < n_groups
    gmax = jnp.max(jnp.where(gmask, lg, neg_inf), axis=-1, keepdims=True)
    gsel = _lane_min_index(gmask & (lg == gmax), lane)
    gsum = jnp.sum(jnp.where(gmask, jnp.exp(lg - gmax), 0.0), axis=-1, keepdims=True)
    group_w = 1.0 / gsum
    lo = n_groups + gsel * per_group
    emask = (lane >= lo) & (lane < lo + per_group)
    m1 = jnp.max(jnp.where(emask, lg, neg_inf), axis=-1, keepdims=True)
    i1 = _lane_min_index(emask & (lg == m1), lane)
    rest = emask & (lane != i1)
    m2 = jnp.max(jnp.where(rest, lg, neg_inf), axis=-1, keepdims=True)
    i2 = _lane_min_index(rest & (lg == m2), lane)
    psum = jnp.sum(jnp.where(emask, jnp.exp(lg - m1), 0.0), axis=-1, keepdims=True)
    p1 = 1.0 / psum
    p2 = jnp.exp(m2 - m1) / psum
    w1 = group_w * (p1 / (p1 + p2))
    w2 = group_w * (p2 / (p1 + p2))
    e1 = (i1 - n_groups).astype(F32)
    e2 = (i2 - n_groups).astype(F32)
    route_ref[...] = jnp.where(lane == 0, e1, jnp.where(lane == 1, e2,
                               jnp.where(lane == 2, w1, jnp.where(lane == 3, w2, 0.0))))


def _merge(x2, ysb, yrw, gate_logits, gate_bias, wsb, wrw, wout, g1, gt1, g2, sc2, sh2, wrt,
           seq, n_groups, per_group):
    n, d = x2.shape
    tm = min(TM_MERGE, seq)
    tpb = seq // tm
    w = ysb.shape[1]
    row = lambda i: (i, 0)
    kernel = functools.partial(_merge_kernel, n_groups=n_groups, per_group=per_group)
    return pl.pallas_call(
        kernel,
        grid=(n // tm,),
        in_specs=[pl.BlockSpec((tm, d), row), pl.BlockSpec((tm, w), row), pl.BlockSpec((tm, w), row),
                  pl.BlockSpec((tm, 2 * d), row), _const_spec((1, 2 * d)),
                  _const_spec(wsb.shape), _const_spec(wrw.shape), _const_spec(wout.shape),
                  _const_spec((1, d)), _batch_spec(d, tpb), _const_spec((1, d)),
                  _batch_spec(d, tpb), _batch_spec(d, tpb), _const_spec(wrt.shape)],
        out_specs=[pl.BlockSpec((tm, d), row),
                   pl.BlockSpec((tm * SUBLANES, LANES), row),
                   pl.BlockSpec((tm, LANES), row)],
        out_shape=[jax.ShapeDtypeStruct((n, d), F32),
                   jax.ShapeDtypeStruct((n * SUBLANES, LANES), F32),
                   jax.ShapeDtypeStruct((n, LANES), F32)],
        compiler_params=_params("parallel"),
        name="merge",
    )(x2, ysb, yrw, gate_logits, gate_bias.reshape(1, -1), wsb, wrw, wout, g1, gt1, g2, sc2, sh2, wrt)


def _tile_rows(ref, index):
    return ref.at[pl.ds(pl.multiple_of(index * SUBLANES, SUBLANES), SUBLANES), :]


def _row_gather(src_hbm, idx_ref, base, dst, sem, n_rows):
    def body(r8, _):
        for u in range(GATHER_UNROLL):
            r = r8 * GATHER_UNROLL + u
            pltpu.make_async_copy(_tile_rows(src_hbm, idx_ref[base + r]), _tile_rows(dst, r), sem).start()
        return 0
    lax.fori_loop(0, n_rows // GATHER_UNROLL, body, 0)


def _row_gather_inline(src_hbm, idx_ref, base, dst, sem, n_rows):
    for r in range(n_rows):
        pltpu.make_async_copy(_tile_rows(src_hbm, idx_ref[base + r]),
                              dst.at[pl.ds(r * SUBLANES, SUBLANES), :], sem).start(priority=r % 2)


def _row_gather_wait(src_hbm, dst, sem):
    pltpu.make_async_copy(src_hbm.at[pl.ds(0, dst.shape[0]), :], dst, sem).wait()


def _expert_kernel(tile_expert_ref, tile_valid_ref, row_tok_ref,
                   h2_hbm, w1_ref, w3_ref, wd_ref,
                   out_ref, gbuf_a, gbuf_b, sems, w1_bf, w3_bf, wd_bf):
    tm = out_ref.shape[0] // SUBLANES
    d = w1_ref.shape[0]
    j = pl.program_id(0)
    even = j % 2 == 0
    valid = tile_valid_ref[j] == 1
    prev_valid = tile_valid_ref[jnp.maximum(j - 1, 0)] == 1

    @pl.when(jnp.logical_or(j == 0, tile_expert_ref[j] != tile_expert_ref[jnp.maximum(j - 1, 0)]))
    def _():
        w1_bf[...] = w1_ref[...].astype(BF16)
        w3_bf[...] = w3_ref[...].astype(BF16)
        wd_bf[...] = wd_ref[...].astype(BF16)

    @pl.when(j == 0)
    def _():
        _row_gather(h2_hbm, row_tok_ref, 0, gbuf_a, sems.at[0], tm)

    def tile(cur, cur_sem, nxt, nxt_sem):
        _row_gather_wait(h2_hbm, cur, cur_sem)
        _row_gather_inline(h2_hbm, row_tok_ref, (j + 1) * tm, nxt, nxt_sem, tm)
        x = jnp.concatenate([cur[pl.ds(c, tm, stride=SUBLANES), :].astype(BF16)
                             for c in range(d // LANES)], axis=1)
        up1 = jnp.dot(x, w1_bf[...], preferred_element_type=F32)
        up3 = jnp.dot(x, w3_bf[...], preferred_element_type=F32)
        hid = up1 * _sigmoid(up1) * up3
        y = jnp.dot(hid.astype(BF16), wd_bf[...], preferred_element_type=F32)
        for c in range(d // LANES):
            out_ref[pl.ds(c, tm, stride=SUBLANES), :] = y[:, c * LANES:(c + 1) * LANES]

    @pl.when(jnp.logical_and(valid, even))
    def _():
        tile(gbuf_a, sems.at[0], gbuf_b, sems.at[1])

    @pl.when(jnp.logical_and(valid, jnp.logical_not(even)))
    def _():
        tile(gbuf_b, sems.at[1], gbuf_a, sems.at[0])

    @pl.when(jnp.logical_not(valid))
    def _():
        drain = jnp.logical_and(j > 0, prev_valid)

        @pl.when(jnp.logical_and(drain, even))
        def _():
            _row_gather_wait(h2_hbm, gbuf_a, sems.at[0])

        @pl.when(jnp.logical_and(drain, jnp.logical_not(even)))
        def _():
            _row_gather_wait(h2_hbm, gbuf_b, sems.at[1])

        out_ref[...] = jnp.zeros_like(out_ref)


def _experts(h2_tiles, tile_expert, tile_valid, row_tok, w1, w3, wd):
    n_tiles = tile_expert.shape[0]
    tm = TM_EXP
    _, d, de = w1.shape
    rows = n_tiles * tm
    grid_spec = pltpu.PrefetchScalarGridSpec(
        num_scalar_prefetch=3,
        grid=(n_tiles,),
        in_specs=[pl.BlockSpec(memory_space=pl.ANY),
                  pl.BlockSpec((None, d, de), lambda j, te, tv, rt: (te[j], 0, 0)),
                  pl.BlockSpec((None, d, de), lambda j, te, tv, rt: (te[j], 0, 0)),
                  pl.BlockSpec((None, de, d), lambda j, te, tv, rt: (te[j], 0, 0))],
        out_specs=pl.BlockSpec((tm * SUBLANES, LANES), lambda j, te, tv, rt: (j, 0)),
        scratch_shapes=[pltpu.VMEM((tm * SUBLANES, LANES), F32), pltpu.VMEM((tm * SUBLANES, LANES), F32),
                        pltpu.SemaphoreType.DMA((2,)),
                        pltpu.VMEM((d, de), BF16), pltpu.VMEM((d, de), BF16), pltpu.VMEM((de, d), BF16)],
    )
    return pl.pallas_call(
        _expert_kernel,
        grid_spec=grid_spec,
        out_shape=jax.ShapeDtypeStruct((rows * SUBLANES, LANES), F32),
        compiler_params=_params("arbitrary"),
        name="experts",
    )(tile_expert, tile_valid, row_tok, h2_tiles, w1, w3, wd)


def _combine_kernel(pos_ref,
                    ye_hbm, x1_ref, route_ref, g3_ref, gt2_ref,
                    out_ref, gbuf_a, gbuf_b, sems, *, n_tokens):
    tm, d = x1_ref.shape
    i = pl.program_id(0)
    nt = pl.num_programs(0)
    even = i % 2 == 0

    @pl.when(i == 0)
    def _():
        for k in range(TOP_K_IN_GROUP):
            _row_gather(ye_hbm, pos_ref, k * n_tokens, gbuf_a.at[k], sems.at[0], tm)

    def tile(cur, cur_sem, nxt, nxt_sem):
        for k in range(TOP_K_IN_GROUP):
            _row_gather_wait(ye_hbm, cur.at[k], cur_sem)
        nxt_tile = jnp.minimum(i + 1, nt - 1)
        for k in range(TOP_K_IN_GROUP):
            _row_gather_inline(ye_hbm, pos_ref, k * n_tokens + nxt_tile * tm, nxt.at[k], nxt_sem, tm)
        wts = [route_ref[:, TOP_K_IN_GROUP + k:TOP_K_IN_GROUP + k + 1] for k in range(TOP_K_IN_GROUP)]
        y = jnp.concatenate(
            [sum(wts[k] * cur[k, pl.ds(c, tm, stride=SUBLANES), :] for k in range(TOP_K_IN_GROUP))
             for c in range(d // LANES)], axis=1)
        out_ref[...] = x1_ref[...] + gt2_ref[...] * (_rms(y) * g3_ref[...])

        @pl.when(i == nt - 1)
        def _():
            for k in range(TOP_K_IN_GROUP):
                _row_gather_wait(ye_hbm, nxt.at[k], nxt_sem)

    @pl.when(even)
    def _():
        tile(gbuf_a, sems.at[0], gbuf_b, sems.at[1])

    @pl.when(jnp.logical_not(even))
    def _():
        tile(gbuf_b, sems.at[1], gbuf_a, sems.at[0])


def _combine(ye_tiles, pos, x1, route, g3, gt2, seq):
    n, d = x1.shape
    tm = min(TM_COMB, seq)
    tpb = seq // tm
    kernel = functools.partial(_combine_kernel, n_tokens=n)
    grid_spec = pltpu.PrefetchScalarGridSpec(
        num_scalar_prefetch=1,
        grid=(n // tm,),
        in_specs=[pl.BlockSpec(memory_space=pl.ANY),
                  pl.BlockSpec((tm, d), lambda i, p: (i, 0)),
                  pl.BlockSpec((tm, LANES), lambda i, p: (i, 0)),
                  pl.BlockSpec((1, d), lambda i, p: (0, 0)),
                  pl.BlockSpec((None, 1, d), lambda i, p: (i // tpb, 0, 0))],
        out_specs=pl.BlockSpec((tm, d), lambda i, p: (i, 0)),
        scratch_shapes=[pltpu.VMEM((TOP_K_IN_GROUP, tm * SUBLANES, LANES), F32),
                        pltpu.VMEM((TOP_K_IN_GROUP, tm * SUBLANES, LANES), F32),
                        pltpu.SemaphoreType.DMA((2,))],
    )
    return pl.pallas_call(
        kernel,
        grid_spec=grid_spec,
        out_shape=jax.ShapeDtypeStruct((n, d), F32),
        compiler_params=_params("arbitrary"),
        name="combine",
    )(pos, ye_tiles, x1, route, g3, gt2)


def _plan_rows(route, n_experts, tm):
    n = route.shape[0]
    pairs = TOP_K_IN_GROUP * n
    expert = jnp.concatenate([route[:, k] for k in range(TOP_K_IN_GROUP)]).astype(jnp.int32)
    token = jnp.tile(jnp.arange(n, dtype=jnp.int32), TOP_K_IN_GROUP)
    onehot = (expert[:, None] == jnp.arange(n_experts, dtype=jnp.int32)[None, :]).astype(jnp.int32)
    rank = jnp.sum(jnp.cumsum(onehot, axis=0) * onehot, axis=1) - 1
    counts = jnp.sum(onehot, axis=0)
    padded = -(-counts // tm) * tm
    ends = jnp.cumsum(padded)
    offsets = ends - padded
    pos = jnp.sum(onehot * offsets[None, :], axis=1) + rank
    n_tiles = pairs // tm + n_experts + 1
    rows = n_tiles * tm
    row_tok = jnp.zeros((rows,), jnp.int32).at[pos].set(token, unique_indices=True, mode="promise_in_bounds")
    tile_start = jnp.arange(n_tiles, dtype=jnp.int32) * tm
    tile_expert = jnp.minimum(jnp.sum((ends[None, :] <= tile_start[:, None]).astype(jnp.int32), axis=1),
                              n_experts - 1)
    tile_valid = (tile_start < ends[-1]).astype(jnp.int32)
    return tile_expert, tile_valid, row_tok, pos.astype(jnp.int32)


def _layer(x, c, w_mod, b_mod, norm_gains, w_in, gate_bias, shift_mu, w0, w_decay_up, a0, w_iclr_up,
           w_gate_up, k_k, k_a, r_k, lnx_w, lnx_b, w_o_sb, w_o_rw, w_out, w_group, w_router,
           w_up1, w_up3, w_down):
    batch, seq, d = x.shape
    n = batch * seq
    sb_width = w_o_sb.shape[0]
    rw_width = w_o_rw.shape[0]
    decay_lora, iclr_lora, gate_lora = w_decay_up.shape[0], w_iclr_up.shape[0], w_gate_up.shape[0]
    lora = decay_lora + iclr_lora + gate_lora
    lora_pad = -(-lora // LANES) * LANES
    n_groups = w_group.shape[1]
    n_experts = w_router.shape[1]
    per_group = n_experts // n_groups

    mod = _modulation(c, w_mod, b_mod)
    sh1, sc1, gt1, sh2, sc2, gt2 = [m.reshape(batch, 1, d) for m in jnp.split(mod, 6, axis=-1)]
    gains = norm_gains.reshape(4, 1, d)

    o1 = 3 * sb_width
    o2 = o1 + 3 * rw_width
    o3 = o2 + lora
    wq = w_in[:, :o1].astype(BF16)
    wr = w_in[:, o1:o2].astype(BF16)
    wl = jnp.zeros((d, lora_pad), BF16).at[:, :lora].set(w_in[:, o2:o3].astype(BF16))
    wg = w_in[:, o3:].astype(BF16)
    x2 = x.reshape(n, d)
    qkv, rkv, lora_cols, gate_logits = _inproj(x2, gains[0], sc1, sh1, wq, wr, wl, wg, seq)

    y_sb = _sb_attention(qkv, batch, seq, sb_width)

    mu_rkv = shift_mu[:3 * rw_width]
    mu_lora = jnp.zeros((lora_pad,), F32).at[:lora].set(shift_mu[3 * rw_width:])
    l1 = decay_lora
    l2 = l1 + iclr_lora
    wd_pad = jnp.zeros((lora_pad, rw_width), F32).at[:l1].set(w_decay_up)
    wa_pad = jnp.zeros((lora_pad, rw_width), F32).at[l1:l2].set(w_iclr_up)
    wg_pad = jnp.zeros((lora_pad, rw_width), F32).at[l2:lora].set(w_gate_up)
    head = jnp.arange(rw_width) // HEAD_DIM
    hsum = (head[:, None] == head[None, :]).astype(BF16)
    q_c, yv_c, bonus, g, m_c, n_c = _rw_chunk(rkv, lora_cols, mu_rkv, mu_lora, w0, wd_pad, a0, wa_pad,
                                              wg_pad, k_k, k_a, r_k.reshape(-1), hsum, seq)
    y_rw = _rw_state(q_c, yv_c, bonus, g, m_c, n_c, lnx_w, lnx_b, hsum, batch, seq)

    wrt = jnp.zeros((d, LANES), F32).at[:, :n_groups].set(w_group)
    wrt = wrt.at[:, n_groups:n_groups + n_experts].set(w_router)
    wrt = jnp.concatenate(_split(wrt), axis=1)
    x1, h2_tiles, route = _merge(x2, y_sb, y_rw, gate_logits, gate_bias, w_o_sb.astype(BF16),
                                 w_o_rw.astype(BF16), w_out.astype(BF16), gains[1], gt1, gains[2],
                                 sc2, sh2, wrt, seq, n_groups, per_group)

    tile_expert, tile_valid, row_tok, pos = _plan_rows(route, n_experts, TM_EXP)
    ye = _experts(h2_tiles, tile_expert, tile_valid, row_tok,
                  w_up1, w_up3, w_down)
    out = _combine(ye, pos, x1, route, gains[3], gt2, seq)
    return out.reshape(batch, seq, d)


def kernel(x, c, w_mod, b_mod, norm_gains, w_in, gate_bias, shift_mu, w0, w_decay_up, a0, w_iclr_up,
           w_gate_up, k_k, k_a, r_k, lnx_w, lnx_b, w_o_sb, w_o_rw, w_out, w_group, w_router,
           w_up1, w_up3, w_down):
    for l in range(w_mod.shape[0]):
        x = _layer(x, c, w_mod[l], b_mod[l], norm_gains[l], w_in[l], gate_bias[l], shift_mu[l], w0[l],
                   w_decay_up[l], a0[l], w_iclr_up[l], w_gate_up[l], k_k[l], k_a[l], r_k[l], lnx_w[l],
                   lnx_b[l], w_o_sb[l], w_o_rw[l], w_out[l], w_group[l], w_router[l], w_up1[l],
                   w_up3[l], w_down[l])
    return x
```

```python
import functools
import math

import jax
import jax.numpy as jnp
from jax import lax
from jax.experimental import pallas as pl
from jax.experimental.pallas import tpu as pltpu

F32 = jnp.float32
BF16 = jnp.bfloat16
HIGHEST = lax.Precision.HIGHEST

HEAD_DIM = 64
TOP_K_IN_GROUP = 2
NORM_EPS = 1e-6
GN_EPS = 64e-5
L2_EPS = 1e-12

LANES = 128
SUBLANES = 8
VMEM_LIMIT_BYTES = 56 * 1024 * 1024

TM_PROJ = 512
T_ATT = 256
ATT_DEAD_LOG2 = -150.0
C_SCAN = 64
TM_CHUNK = 256
TM_MERGE = 512
TM_EXP = 256
TM_COMB = 256
GATHER_UNROLL = 8
GATHER_AHEAD = 2
GATHER_BUFS = GATHER_AHEAD + 1

_NN = ((1,), (0,))
_NT = ((1,), (1,))
_TN = ((0,), (0,))


def _params(*sem):
    return pltpu.CompilerParams(dimension_semantics=sem, vmem_limit_bytes=VMEM_LIMIT_BYTES)


def _hdot(a, b):
    return jnp.dot(a, b, precision=HIGHEST, preferred_element_type=F32)


def _split(x):
    hi = x.astype(BF16)
    return hi, (x - hi.astype(F32)).astype(BF16)


def _mm1(a, b, dims=_NN):
    return lax.dot_general(a.astype(BF16), b.astype(BF16), (dims, ((), ())), preferred_element_type=F32)


def _mm01(a, b, exact):
    if exact == "a":
        hi, lo = _split(b)
        return jnp.dot(a, hi, preferred_element_type=F32) + jnp.dot(a, lo, preferred_element_type=F32)
    hi, lo = _split(a)
    return jnp.dot(hi, b, preferred_element_type=F32) + jnp.dot(lo, b, preferred_element_type=F32)


def _rms(x):
    return x * lax.rsqrt(jnp.mean(x * x, axis=-1, keepdims=True) + NORM_EPS)


def _sigmoid(x):
    return 1.0 / (1.0 + jnp.exp(-x))


def _mod_kernel(c_ref, w_ref, b_ref, o_ref):
    c = c_ref[...]
    o_ref[...] = _hdot(c * _sigmoid(c), w_ref[...]) + b_ref[...]


def _modulation(c, w_mod, b_mod):
    b, d = c.shape
    n_out = w_mod.shape[1]
    rows = -(-b // SUBLANES) * SUBLANES
    c_pad = jnp.zeros((rows, d), F32).at[:b].set(c)
    out = pl.pallas_call(
        _mod_kernel,
        grid=(n_out // d,),
        in_specs=[pl.BlockSpec((rows, d), lambda j: (0, 0)),
                  pl.BlockSpec((d, d), lambda j: (0, j)),
                  pl.BlockSpec((1, d), lambda j: (0, j))],
        out_specs=pl.BlockSpec((rows, d), lambda j: (0, j)),
        out_shape=jax.ShapeDtypeStruct((rows, n_out), F32),
        compiler_params=_params("parallel"),
        name="mod",
    )(c_pad, w_mod, b_mod.reshape(1, n_out))
    return out[:b]


def _inproj_kernel(x_ref, g_ref, sc_ref, sh_ref, wq_ref, wr_ref, wl_ref, wg_ref,
                   qkv_ref, rkv_ref, lora_ref, gate_ref):
    h = _rms(x_ref[...]) * g_ref[...] * (1.0 + sc_ref[...]) + sh_ref[...]
    hb = h.astype(BF16)
    qkv_ref[...] = jnp.dot(hb, wq_ref[...], preferred_element_type=F32).astype(qkv_ref.dtype)
    rkv_ref[...] = jnp.dot(hb, wr_ref[...], preferred_element_type=F32)
    lora_ref[...] = jnp.dot(hb, wl_ref[...], preferred_element_type=F32)
    gate_ref[...] = jnp.dot(hb, wg_ref[...], preferred_element_type=F32)


def _const_spec(shape):
    return pl.BlockSpec(shape, lambda *_: (0,) * len(shape))


def _batch_spec(d, tiles_per_batch):
    return pl.BlockSpec((None, 1, d), lambda i: (i // tiles_per_batch, 0, 0))


def _inproj(x2, gain, sc, sh, wq, wr, wl, wg, seq):
    n, d = x2.shape
    tm = min(TM_PROJ, seq)
    tpb = seq // tm
    outs = [(wq.shape[1], BF16), (wr.shape[1], F32), (wl.shape[1], F32), (wg.shape[1], F32)]
    return pl.pallas_call(
        _inproj_kernel,
        grid=(n // tm,),
        in_specs=[pl.BlockSpec((tm, d), lambda i: (i, 0)),
                  _const_spec((1, d)), _batch_spec(d, tpb), _batch_spec(d, tpb),
                  _const_spec(wq.shape), _const_spec(wr.shape), _const_spec(wl.shape),
                  _const_spec(wg.shape)],
        out_specs=[pl.BlockSpec((tm, w), lambda i: (i, 0)) for w, _ in outs],
        out_shape=[jax.ShapeDtypeStruct((n, w), dt) for w, dt in outs],
        compiler_params=_params("parallel"),
        name="inproj",
    )(x2, gain, sc, sh, wq, wr, wl, wg)


def _sb_attn_kernel(q_ref, k_ref, v_ref, o_ref, acc_ref, *, scale):
    t = q_ref.shape[0]
    qi = pl.program_id(2)
    row = lax.broadcasted_iota(jnp.int32, (t, t), 0)
    col = lax.broadcasted_iota(jnp.int32, (t, t), 1)
    tri = jnp.where(row > col, 1.0, 0.0).astype(BF16)
    strict = col < row
    lane = lax.broadcasted_iota(jnp.int32, (t, LANES), 1)
    first = lane < HEAD_DIM
    q2 = q_ref[...].astype(F32) * (scale * math.log2(math.e))
    qs = (jnp.where(first, q2, 0.0).astype(BF16), jnp.where(first, 0.0, q2).astype(BF16))

    def blocks(kbs, runs, masked):
        k2 = [k_ref[pl.ds(pl.multiple_of(kb * t, t), t), :] for kb in kbs]
        v2 = [v_ref[pl.ds(pl.multiple_of(kb * t, t), t), :] for kb in kbs]
        z = [[lax.dot_general(qs[hh], k2[j], (_NT, ((), ())), preferred_element_type=F32) for hh in range(2)]
             for j in range(len(kbs))]
        runs = list(runs)
        for j in range(len(kbs)):
            for hh in range(2):
                zz = z[j][hh]
                neg_part = jnp.minimum(zz, 0.0)
                neg_other = neg_part - zz
                soft = jnp.log2(1.0 + jnp.exp2(neg_part + neg_other))
                log_beta = neg_part - soft
                log_keep = neg_other - soft
                if masked[j]:
                    log_keep = jnp.where(strict, log_keep, 0.0)
                later = jnp.dot(log_keep.astype(BF16), tri, preferred_element_type=F32)
                att = jnp.exp2(log_beta + later + runs[hh])
                if masked[j]:
                    att = jnp.where(strict, att, 0.0)
                acc_ref[hh] += jnp.dot(att.astype(BF16), v2[j], preferred_element_type=F32)
                runs[hh] = runs[hh] + jnp.sum(log_keep, axis=-1, keepdims=True)
        return tuple(runs)

    acc_ref[...] = jnp.zeros_like(acc_ref)
    zero = jnp.zeros((t, 1), F32)
    runs = lax.cond(qi > 0,
                    lambda: blocks([qi, qi - 1], (zero, zero), (True, False)),
                    lambda: blocks([qi], (zero, zero), (True,)))

    def live(state):
        kb, r = state
        return jnp.logical_and(kb >= 0, jnp.max(jnp.maximum(r[0], r[1])) > ATT_DEAD_LOG2)

    def body(state):
        kb, r = state
        return kb - 1, blocks([kb], r, (False,))

    lax.while_loop(live, body, (qi - 2, runs))
    o_ref[...] = jnp.where(first, acc_ref[0], acc_ref[1]).astype(o_ref.dtype)


def _sb_attention(qkv, batch, seq, width):
    n = qkv.shape[0]
    t = min(T_ATT, seq)
    n_pairs = width // LANES
    nq = seq // t
    kernel = functools.partial(_sb_attn_kernel, scale=1.0 / math.sqrt(HEAD_DIM))
    return pl.pallas_call(
        kernel,
        grid=(batch, n_pairs, nq),
        in_specs=[pl.BlockSpec((t, LANES), lambda b, p, i: (b * nq + i, p)),
                  pl.BlockSpec((seq, LANES), lambda b, p, i: (b, n_pairs + p)),
                  pl.BlockSpec((seq, LANES), lambda b, p, i: (b, 2 * n_pairs + p))],
        out_specs=pl.BlockSpec((t, LANES), lambda b, p, i: (b * nq + i, p)),
        out_shape=jax.ShapeDtypeStruct((n, width), BF16),
        scratch_shapes=[pltpu.VMEM((2, t, LANES), F32)],
        compiler_params=_params("parallel", "parallel", "parallel"),
        name="sb_attn",
    )(qkv, qkv, qkv)


def _shift_rows(p, halo_row):
    rolled = pltpu.roll(p, 1, 0)
    row = lax.broadcasted_iota(jnp.int32, p.shape, 0)
    return jnp.where(row == 0, halo_row, rolled)


def _rw_chunk_kernel(rkv_ref, rkv_halo_ref, lora_ref, lora_halo_ref, mu_rkv_ref, mu_lora_ref,
                     w0_ref, wd_ref, a0_ref, wa_ref, wg_ref, kk_ref, ka_ref, rk_ref, hsum_ref,
                     q_out, yv_out, bonus_out, g_out, m_out, n_out, *, tiles_per_batch, chunk):
    i = pl.program_id(0)
    keep = jnp.where(i % tiles_per_batch == 0, 0.0, 1.0)
    w = w0_ref.shape[1]
    tm = rkv_ref.shape[0]
    c = chunk
    hd = HEAD_DIM
    n_heads = w // hd

    p = rkv_ref[...]
    prev = _shift_rows(p, rkv_halo_ref[SUBLANES - 1:SUBLANES, :] * keep)
    xs = p + mu_rkv_ref[...] * (prev - p)
    pl_ = lora_ref[...]
    prevl = _shift_rows(pl_, lora_halo_ref[SUBLANES - 1:SUBLANES, :] * keep)
    xl = pl_ + mu_lora_ref[...] * (prevl - pl_)

    r = xs[:, :w]
    k = xs[:, w:2 * w]
    v = xs[:, 2 * w:]
    dec_in = w0_ref[...] + _mm1(jnp.tanh(xl), wd_ref[...])
    neg = -dec_in
    softplus = jnp.maximum(neg, 0.0) + jnp.log(1.0 + jnp.exp(-jnp.abs(neg)))
    lw = -jnp.exp(-softplus - 0.5)
    a_gate = _sigmoid(a0_ref[...] + _mm1(xl, wa_ref[...]))
    g_out[...] = _mm1(_sigmoid(xl), wg_ref[...])
    kk = k * kk_ref[...]
    k = k * (1.0 + (a_gate - 1.0) * ka_ref[...])
    hsum = hsum_ref[...]
    kkn = kk / jnp.maximum(jnp.sqrt(_mm01(kk * kk, hsum, "b")), L2_EPS)
    a = -kkn
    b = kkn * a_gate
    bonus_out[...] = _mm01(r * k * rk_ref[...], hsum, "b") * v

    row = lax.broadcasted_iota(jnp.int32, (c, c), 0)
    col = lax.broadcasted_iota(jnp.int32, (c, c), 1)
    incl = col <= row
    strict = col < row
    tril = jnp.where(incl, 1.0, 0.0).astype(BF16)
    eye = jnp.where(row == col, 1.0, 0.0)

    a_t, r_t, b_t, k_t, b_h, k_h, v_p, g_c = [], [], [], [], [], [], [], []
    for g in range(tm // c):
        rs = slice(g * c, (g + 1) * c)
        lw_c = lw[rs]
        cum = _mm01(tril, lw_c, "a")
        total = cum[c - 1:c, :]
        inv = jnp.exp(-cum)
        tail = jnp.exp(total - cum)
        full = {"a_t": a[rs] * jnp.exp(cum - lw_c), "r_t": r[rs] * jnp.exp(cum), "b_t": b[rs] * inv,
                "k_t": k[rs] * inv, "b_h": b[rs] * tail, "k_h": k[rs] * tail, "v": v[rs],
                "g_c": jnp.exp(total)}
        for h in range(n_heads):
            sl = slice(h * hd, (h + 1) * hd)
            a_t.append(full["a_t"][:, sl]); r_t.append(full["r_t"][:, sl])
            b_t.append(full["b_t"][:, sl]); k_t.append(full["k_t"][:, sl])
            b_h.append(full["b_h"][:, sl]); k_h.append(full["k_h"][:, sl])
            v_p.append(full["v"][:, sl]); g_c.append(full["g_c"][:, sl])
    n_prob = len(a_t)
    prob = range(n_prob)

    lhs = [jnp.concatenate([a_t[j], r_t[j]], axis=0) for j in prob]
    gram_b = [_mm1(lhs[j], b_t[j], _NT) for j in prob]
    gram_k = [_mm1(lhs[j], k_t[j], _NT) for j in prob]
    a_ab = [jnp.where(strict, gram_b[j][:c], 0.0) for j in prob]
    a_rb = [jnp.where(incl, gram_b[j][c:], 0.0) for j in prob]
    a_ak = [jnp.where(strict, gram_k[j][:c], 0.0) for j in prob]
    a_rk = [jnp.where(incl, gram_k[j][c:], 0.0) for j in prob]
    ak_v = [_mm1(a_ak[j], v_p[j]) for j in prob]
    rk_v = [_mm1(a_rk[j], v_p[j]) for j in prob]
    wu = [jnp.concatenate([a_t[j], ak_v[j]], axis=1) for j in prob]
    power = a_ab
    n_apply = max(1, math.ceil(math.log2(c)))
    for step in range(n_apply):
        wu = [wu[j] + _mm1(power[j], wu[j]) for j in prob]
        if step + 1 < n_apply:
            power = [_mm1(power[j], power[j]) for j in prob]
    qy = [_mm1(a_rb[j], wu[j]) + jnp.concatenate([r_t[j], rk_v[j]], axis=1) for j in prob]
    wu_b = [_mm1(wu[j], b_h[j], _TN) for j in prob]
    v_k = [_mm1(v_p[j], k_h[j], _TN) for j in prob]
    for j in prob:
        g, h = divmod(j, n_heads)
        rs = slice(g * c, (g + 1) * c)
        sl = slice(h * hd, (h + 1) * hd)
        q_out[rs, sl] = qy[j][:, :hd]
        yv_out[rs, sl] = qy[j][:, hd:]
        m_out[g, h] = wu_b[j][:hd] + eye * g_c[j]
        n_out[g, h] = wu_b[j][hd:] + v_k[j]


def _rw_chunk(rkv, lora, mu_rkv, mu_lora, w0, wd_pad, a0, wa_pad, wg_pad, k_k, k_a, r_k, hsum, seq):
    n, w3 = rkv.shape
    w = w3 // 3
    wl = lora.shape[1]
    c = min(C_SCAN, seq)
    tm = min(TM_CHUNK, seq)
    tpb = seq // tm
    cpt = tm // c
    hb = tm // SUBLANES
    n_heads = w // HEAD_DIM
    kernel = functools.partial(_rw_chunk_kernel, tiles_per_batch=tpb, chunk=c)
    row = lambda i: (i, 0)
    halo = lambda i: (jnp.maximum(i * hb - 1, 0), 0)
    vec = lambda a: a.reshape(1, -1)
    mat = pl.BlockSpec((cpt, n_heads, HEAD_DIM, HEAD_DIM), lambda i: (i, 0, 0, 0))
    mat_shape = jax.ShapeDtypeStruct((n // c, n_heads, HEAD_DIM, HEAD_DIM), F32)
    return pl.pallas_call(
        kernel,
        grid=(n // tm,),
        in_specs=[pl.BlockSpec((tm, w3), row), pl.BlockSpec((SUBLANES, w3), halo),
                  pl.BlockSpec((tm, wl), row), pl.BlockSpec((SUBLANES, wl), halo),
                  _const_spec((1, w3)), _const_spec((1, wl)),
                  _const_spec((1, w)), _const_spec((wl, w)), _const_spec((1, w)),
                  _const_spec((wl, w)), _const_spec((wl, w)),
                  _const_spec((1, w)), _const_spec((1, w)), _const_spec((1, w)),
                  _const_spec((w, w))],
        out_specs=[pl.BlockSpec((tm, w), row)] * 4 + [mat, mat],
        out_shape=[jax.ShapeDtypeStruct((n, w), F32)] * 4 + [mat_shape, mat_shape],
        compiler_params=_params("parallel"),
        name="rw_chunk",
    )(rkv, rkv, lora, lora, vec(mu_rkv), vec(mu_lora), vec(w0), wd_pad, vec(a0), wa_pad, wg_pad,
      vec(k_k), vec(k_a), vec(r_k), hsum)


def _rw_state_kernel(q_ref, yv_ref, bonus_ref, g_ref, m_ref, n_ref, lnw_ref, lnb_ref, hsum_ref,
                     o_ref, state_ref):
    n_batch, c, w = q_ref.shape
    hd = HEAD_DIM
    n_heads = w // hd

    @pl.when(pl.program_id(0) == 0)
    def _():
        state_ref[...] = jnp.zeros_like(state_ref)

    pairs = [(bi, h) for bi in range(n_batch) for h in range(n_heads)]
    state = [state_ref[bi, h] for bi, h in pairs]
    y = [_mm1(q_ref[bi, :, h * hd:(h + 1) * hd], state[j], _NT) + yv_ref[bi, :, h * hd:(h + 1) * hd]
         for j, (bi, h) in enumerate(pairs)]
    new_state = [_mm1(state[j], m_ref[bi, h]) + n_ref[bi, h] for j, (bi, h) in enumerate(pairs)]
    for j, (bi, h) in enumerate(pairs):
        state_ref[bi, h] = new_state[j]

    hsum = hsum_ref[...]
    for bi in range(n_batch):
        y_b = jnp.concatenate(y[bi * n_heads:(bi + 1) * n_heads], axis=1)
        mean = _mm01(y_b, hsum, "b") * (1.0 / hd)
        dev = y_b - mean
        var = _mm01(dev * dev, hsum, "b") * (1.0 / hd)
        yn = dev * lax.rsqrt(var + GN_EPS) * lnw_ref[...] + lnb_ref[...]
        o_ref[bi] = ((yn + bonus_ref[bi]) * g_ref[bi]).astype(o_ref.dtype)


def _rw_state(q, yv, bonus, g, m, n_mat, lnx_w, lnx_b, hsum, batch, seq):
    w = q.shape[1]
    c = min(C_SCAN, seq)
    nc = seq // c
    n_heads = w // HEAD_DIM
    rows = pl.BlockSpec((batch, c, w), lambda ci: (0, ci, 0))
    mats = pl.BlockSpec((batch, None, n_heads, HEAD_DIM, HEAD_DIM), lambda ci: (0, ci, 0, 0, 0))
    to3 = lambda a: a.reshape(batch, seq, w)
    to5 = lambda a: a.reshape(batch, nc, n_heads, HEAD_DIM, HEAD_DIM)
    out = pl.pallas_call(
        _rw_state_kernel,
        grid=(nc,),
        in_specs=[rows] * 4 + [mats, mats, _const_spec((1, w)), _const_spec((1, w)), _const_spec((w, w))],
        out_specs=rows,
        out_shape=jax.ShapeDtypeStruct((batch, seq, w), BF16),
        scratch_shapes=[pltpu.VMEM((batch, n_heads, HEAD_DIM, HEAD_DIM), F32)],
        compiler_params=_params("arbitrary"),
        name="rw_state",
    )(to3(q), to3(yv), to3(bonus), to3(g), to5(m), to5(n_mat), lnx_w.reshape(1, w), lnx_b.reshape(1, w), hsum)
    return out.reshape(batch * seq, w)


def _lane_min_index(mask, lane):
    return jnp.min(jnp.where(mask, lane, 4 * LANES), axis=-1, keepdims=True)


def _merge_kernel(x_ref, ysb_ref, yrw_ref, gl_ref, gb_ref, wsb_ref, wrw_ref, wout_ref,
                  g1_ref, gt1_ref, g2_ref, sc2_ref, sh2_ref, wrt_ref,
                  x1_ref, h2_ref, route_ref, *, n_groups, per_group):
    d = x_ref.shape[1]
    tm = x_ref.shape[0]
    gates = _sigmoid(gl_ref[...] + gb_ref[...])
    merged = (gates[:, :d] * jnp.dot(ysb_ref[...], wsb_ref[...], preferred_element_type=F32)
              + gates[:, d:] * jnp.dot(yrw_ref[...], wrw_ref[...], preferred_element_type=F32))
    y = jnp.dot(merged.astype(BF16), wout_ref[...], preferred_element_type=F32)
    x1 = x_ref[...] + gt1_ref[...] * (_rms(y) * g1_ref[...])
    x1_ref[...] = x1
    h2 = _rms(x1) * g2_ref[...] * (1.0 + sc2_ref[...]) + sh2_ref[...]
    for j in range(d // LANES):
        h2_ref[pl.ds(j, tm, stride=SUBLANES), :] = h2[:, j * LANES:(j + 1) * LANES]

    h_hi, h_lo = _split(h2)
    hi_part = jnp.dot(h_hi, wrt_ref[...], preferred_element_type=F32)
    lg = hi_part[:, :LANES] + (hi_part[:, LANES:] + jnp.dot(h_lo, wrt_ref[:, :LANES], preferred_element_type=F32))
    lane = lax.broadcasted_iota(jnp.int32, lg.shape, 1)
    neg_inf = -jnp.inf
    gmask = lane < n_groups
    gmax = jnp.max(jnp.where(gmask, lg, neg_inf), axis=-1, keepdims=True)
    gsel = _lane_min_index(gmask & (lg == gmax), lane)
    gsum = jnp.sum(jnp.where(gmask, jnp.exp(lg - gmax), 0.0), axis=-1, keepdims=True)
    group_w = 1.0 / gsum
    lo = n_groups + gsel * per_group
    emask = (lane >= lo) & (lane < lo + per_group)
    m1 = jnp.max(jnp.where(emask, lg, neg_inf), axis=-1, keepdims=True)
    i1 = _lane_min_index(emask & (lg == m1), lane)
    rest = emask & (lane != i1)
    m2 = jnp.max(jnp.where(rest, lg, neg_inf), axis=-1, keepdims=True)
    i2 = _lane_min_index(rest & (lg == m2), lane)
    psum = jnp.sum(jnp.where(emask, jnp.exp(lg - m1), 0.0), axis=-1, keepdims=True)
    p1 = 1.0 / psum
    p2 = jnp.exp(m2 - m1) / psum
    w1 = group_w * (p1 / (p1 + p2))
    w2 = group_w * (p2 / (p1 + p2))
    e1 = (i1 - n_groups).astype(F32)
    e2 = (i2 - n_groups).astype(F32)
    route_ref[...] = jnp.where(lane == 0, e1, jnp.where(lane == 1, e2,
                               jnp.where(lane == 2, w1, jnp.where(lane == 3, w2, 0.0))))


def _merge(x2, ysb, yrw, gate_logits, gate_bias, wsb, wrw, wout, g1, gt1, g2, sc2, sh2, wrt,
           seq, n_groups, per_group):
    n, d = x2.shape
    tm = min(TM_MERGE, seq)
    tpb = seq // tm
    w = ysb.shape[1]
    row = lambda i: (i, 0)
    kernel = functools.partial(_merge_kernel, n_groups=n_groups, per_group=per_group)
    return pl.pallas_call(
        kernel,
        grid=(n // tm,),
        in_specs=[pl.BlockSpec((tm, d), row), pl.BlockSpec((tm, w), row), pl.BlockSpec((tm, w), row),
                  pl.BlockSpec((tm, 2 * d), row), _const_spec((1, 2 * d)),
                  _const_spec(wsb.shape), _const_spec(wrw.shape), _const_spec(wout.shape),
                  _const_spec((1, d)), _batch_spec(d, tpb), _const_spec((1, d)),
                  _batch_spec(d, tpb), _batch_spec(d, tpb), _const_spec(wrt.shape)],
        out_specs=[pl.BlockSpec((tm, d), row),
                   pl.BlockSpec((tm * SUBLANES, LANES), row),
                   pl.BlockSpec((tm, LANES), row)],
        out_shape=[jax.ShapeDtypeStruct((n, d), F32),
                   jax.ShapeDtypeStruct((n * SUBLANES, LANES), F32),
                   jax.ShapeDtypeStruct((n, LANES), F32)],
        compiler_params=_params("parallel"),
        name="merge",
    )(x2, ysb, yrw, gate_logits, gate_bias.reshape(1, -1), wsb, wrw, wout, g1, gt1, g2, sc2, sh2, wrt)


def _tile_rows(ref, index):
    return ref.at[pl.ds(pl.multiple_of(index * SUBLANES, SUBLANES), SUBLANES), :]


def _row_gather(src_hbm, idx_ref, base, dst, sem, n_rows):
    def body(r8, _):
        for u in range(GATHER_UNROLL):
            r = r8 * GATHER_UNROLL + u
            pltpu.make_async_copy(_tile_rows(src_hbm, idx_ref[base + r]), _tile_rows(dst, r), sem).start()
        return 0
    lax.fori_loop(0, n_rows // GATHER_UNROLL, body, 0)


def _row_gather_inline(src_hbm, idx_ref, base, dst, sem, n_rows):
    for r in range(n_rows):
        pltpu.make_async_copy(_tile_rows(src_hbm, idx_ref[base + r]),
                              dst.at[pl.ds(r * SUBLANES, SUBLANES), :], sem).start(priority=r % 2)


def _row_gather_wait(src_hbm, dst, sem):
    pltpu.make_async_copy(src_hbm.at[pl.ds(0, dst.shape[0]), :], dst, sem).wait()


def _expert_kernel(tile_expert_ref, tile_valid_ref, row_tok_ref,
                   h2_hbm, w1_ref, w3_ref, wd_ref,
                   out_ref, gbuf_0, gbuf_1, gbuf_2, sems, w1_bf, w3_bf, wd_bf):
    tm = out_ref.shape[0] // SUBLANES
    d = w1_ref.shape[0]
    j = pl.program_id(0)
    phase = j % GATHER_BUFS
    bufs = (gbuf_0, gbuf_1, gbuf_2)
    valid = tile_valid_ref[j] == 1
    prev_valid = tile_valid_ref[jnp.maximum(j - 1, 0)] == 1

    @pl.when(jnp.logical_or(j == 0, tile_expert_ref[j] != tile_expert_ref[jnp.maximum(j - 1, 0)]))
    def _():
        w1_bf[...] = w1_ref[...].astype(BF16)
        w3_bf[...] = w3_ref[...].astype(BF16)
        wd_bf[...] = wd_ref[...].astype(BF16)

    @pl.when(j == 0)
    def _():
        for t in range(GATHER_AHEAD):
            _row_gather(h2_hbm, row_tok_ref, t * tm, bufs[t], sems.at[t], tm)

    def tile(p):
        cur = bufs[p]
        ahead = (p + GATHER_AHEAD) % GATHER_BUFS
        _row_gather_wait(h2_hbm, cur, sems.at[p])
        _row_gather_inline(h2_hbm, row_tok_ref, (j + GATHER_AHEAD) * tm, bufs[ahead], sems.at[ahead], tm)
        x = jnp.concatenate([cur[pl.ds(c, tm, stride=SUBLANES), :].astype(BF16)
                             for c in range(d // LANES)], axis=1)
        up1 = jnp.dot(x, w1_bf[...], preferred_element_type=F32)
        up3 = jnp.dot(x, w3_bf[...], preferred_element_type=F32)
        hid = up1 * _sigmoid(up1) * up3
        y = jnp.dot(hid.astype(BF16), wd_bf[...], preferred_element_type=F32)
        for c in range(d // LANES):
            out_ref[pl.ds(c, tm, stride=SUBLANES), :] = y[:, c * LANES:(c + 1) * LANES]

    for p in range(GATHER_BUFS):
        pl.when(jnp.logical_and(valid, phase == p))(functools.partial(tile, p))

    @pl.when(jnp.logical_not(valid))
    def _():
        drain = jnp.logical_and(j > 0, prev_valid)
        for p in range(GATHER_BUFS):
            @pl.when(jnp.logical_and(drain, phase == p))
            def _():
                for t in range(GATHER_AHEAD):
                    q = (p + t) % GATHER_BUFS
                    _row_gather_wait(h2_hbm, bufs[q], sems.at[q])

        out_ref[...] = jnp.zeros_like(out_ref)


def _experts(h2_tiles, tile_expert, tile_valid, row_tok, w1, w3, wd):
    n_tiles = tile_expert.shape[0]
    tm = TM_EXP
    _, d, de = w1.shape
    rows = n_tiles * tm
    grid_spec = pltpu.PrefetchScalarGridSpec(
        num_scalar_prefetch=3,
        grid=(n_tiles,),
        in_specs=[pl.BlockSpec(memory_space=pl.ANY),
                  pl.BlockSpec((None, d, de), lambda j, te, tv, rt: (te[j], 0, 0)),
                  pl.BlockSpec((None, d, de), lambda j, te, tv, rt: (te[j], 0, 0)),
                  pl.BlockSpec((None, de, d), lambda j, te, tv, rt: (te[j], 0, 0))],
        out_specs=pl.BlockSpec((tm * SUBLANES, LANES), lambda j, te, tv, rt: (j, 0)),
        scratch_shapes=[pltpu.VMEM((tm * SUBLANES, LANES), F32) for _ in range(GATHER_BUFS)]
                       + [pltpu.SemaphoreType.DMA((GATHER_BUFS,)),
                        pltpu.VMEM((d, de), BF16), pltpu.VMEM((d, de), BF16), pltpu.VMEM((de, d), BF16)],
    )
    return pl.pallas_call(
        _expert_kernel,
        grid_spec=grid_spec,
        out_shape=jax.ShapeDtypeStruct((rows * SUBLANES, LANES), F32),
        compiler_params=_params("arbitrary"),
        name="experts",
    )(tile_expert, tile_valid, row_tok, h2_tiles, w1, w3, wd)


def _combine_kernel(pos_ref,
                    ye_hbm, x1_ref, route_ref, g3_ref, gt2_ref,
                    out_ref, gbuf_0, gbuf_1, gbuf_2, sems, *, n_tokens):
    tm, d = x1_ref.shape
    i = pl.program_id(0)
    nt = pl.num_programs(0)
    phase = i % GATHER_BUFS
    bufs = (gbuf_0, gbuf_1, gbuf_2)

    def fetch(tile_index, buf, sem, issue):
        for k in range(TOP_K_IN_GROUP):
            issue(ye_hbm, pos_ref, k * n_tokens + tile_index * tm, buf.at[k], sem, tm)

    @pl.when(i == 0)
    def _():
        for t in range(GATHER_AHEAD):
            fetch(jnp.minimum(t, nt - 1), bufs[t], sems.at[t], _row_gather)

    def tile(p):
        cur = bufs[p]
        for k in range(TOP_K_IN_GROUP):
            _row_gather_wait(ye_hbm, cur.at[k], sems.at[p])
        ahead = (p + GATHER_AHEAD) % GATHER_BUFS
        fetch(jnp.minimum(i + GATHER_AHEAD, nt - 1), bufs[ahead], sems.at[ahead], _row_gather_inline)
        wts = [route_ref[:, TOP_K_IN_GROUP + k:TOP_K_IN_GROUP + k + 1] for k in range(TOP_K_IN_GROUP)]
        y = jnp.concatenate(
            [sum(wts[k] * cur[k, pl.ds(c, tm, stride=SUBLANES), :] for k in range(TOP_K_IN_GROUP))
             for c in range(d // LANES)], axis=1)
        out_ref[...] = x1_ref[...] + gt2_ref[...] * (_rms(y) * g3_ref[...])

        @pl.when(i == nt - 1)
        def _():
            for t in range(1, GATHER_BUFS):
                q = (p + t) % GATHER_BUFS
                for k in range(TOP_K_IN_GROUP):
                    _row_gather_wait(ye_hbm, bufs[q].at[k], sems.at[q])

    for p in range(GATHER_BUFS):
        pl.when(phase == p)(functools.partial(tile, p))


def _combine(ye_tiles, pos, x1, route, g3, gt2, seq):
    n, d = x1.shape
    tm = min(TM_COMB, seq)
    tpb = seq // tm
    kernel = functools.partial(_combine_kernel, n_tokens=n)
    grid_spec = pltpu.PrefetchScalarGridSpec(
        num_scalar_prefetch=1,
        grid=(n // tm,),
        in_specs=[pl.BlockSpec(memory_space=pl.ANY),
                  pl.BlockSpec((tm, d), lambda i, p: (i, 0)),
                  pl.BlockSpec((tm, LANES), lambda i, p: (i, 0)),
                  pl.BlockSpec((1, d), lambda i, p: (0, 0)),
                  pl.BlockSpec((None, 1, d), lambda i, p: (i // tpb, 0, 0))],
        out_specs=pl.BlockSpec((tm, d), lambda i, p: (i, 0)),
        scratch_shapes=[pltpu.VMEM((TOP_K_IN_GROUP, tm * SUBLANES, LANES), F32) for _ in range(GATHER_BUFS)]
                       + [pltpu.SemaphoreType.DMA((GATHER_BUFS,))],
    )
    return pl.pallas_call(
        kernel,
        grid_spec=grid_spec,
        out_shape=jax.ShapeDtypeStruct((n, d), F32),
        compiler_params=_params("arbitrary"),
        name="combine",
    )(pos, ye_tiles, x1, route, g3, gt2)


def _plan_rows(route, n_experts, tm):
    n = route.shape[0]
    pairs = TOP_K_IN_GROUP * n
    expert = jnp.concatenate([route[:, k] for k in range(TOP_K_IN_GROUP)]).astype(jnp.int32)
    token = jnp.tile(jnp.arange(n, dtype=jnp.int32), TOP_K_IN_GROUP)
    onehot = (expert[:, None] == jnp.arange(n_experts, dtype=jnp.int32)[None, :]).astype(jnp.int32)
    rank = jnp.sum(jnp.cumsum(onehot, axis=0) * onehot, axis=1) - 1
    counts = jnp.sum(onehot, axis=0)
    padded = -(-counts // tm) * tm
    ends = jnp.cumsum(padded)
    offsets = ends - padded
    pos = jnp.sum(onehot * offsets[None, :], axis=1) + rank
    n_tiles = pairs // tm + n_experts + GATHER_AHEAD
    rows = n_tiles * tm
    row_tok = jnp.zeros((rows,), jnp.int32).at[pos].set(token, unique_indices=True, mode="promise_in_bounds")
    tile_start = jnp.arange(n_tiles, dtype=jnp.int32) * tm
    tile_expert = jnp.minimum(jnp.sum((ends[None, :] <= tile_start[:, None]).astype(jnp.int32), axis=1),
                              n_experts - 1)
    tile_valid = (tile_start < ends[-1]).astype(jnp.int32)
    return tile_expert, tile_valid, row_tok, pos.astype(jnp.int32)


def _layer(x, c, w_mod, b_mod, norm_gains, w_in, gate_bias, shift_mu, w0, w_decay_up, a0, w_iclr_up,
           w_gate_up, k_k, k_a, r_k, lnx_w, lnx_b, w_o_sb, w_o_rw, w_out, w_group, w_router,
           w_up1, w_up3, w_down):
    batch, seq, d = x.shape
    n = batch * seq
    sb_width = w_o_sb.shape[0]
    rw_width = w_o_rw.shape[0]
    decay_lora, iclr_lora, gate_lora = w_decay_up.shape[0], w_iclr_up.shape[0], w_gate_up.shape[0]
    lora = decay_lora + iclr_lora + gate_lora
    lora_pad = -(-lora // LANES) * LANES
    n_groups = w_group.shape[1]
    n_experts = w_router.shape[1]
    per_group = n_experts // n_groups

    mod = _modulation(c, w_mod, b_mod)
    sh1, sc1, gt1, sh2, sc2, gt2 = [m.reshape(batch, 1, d) for m in jnp.split(mod, 6, axis=-1)]
    gains = norm_gains.reshape(4, 1, d)

    o1 = 3 * sb_width
    o2 = o1 + 3 * rw_width
    o3 = o2 + lora
    wq = w_in[:, :o1].astype(BF16)
    wr = w_in[:, o1:o2].astype(BF16)
    wl = jnp.zeros((d, lora_pad), BF16).at[:, :lora].set(w_in[:, o2:o3].astype(BF16))
    wg = w_in[:, o3:].astype(BF16)
    x2 = x.reshape(n, d)
    qkv, rkv, lora_cols, gate_logits = _inproj(x2, gains[0], sc1, sh1, wq, wr, wl, wg, seq)

    y_sb = _sb_attention(qkv, batch, seq, sb_width)

    mu_rkv = shift_mu[:3 * rw_width]
    mu_lora = jnp.zeros((lora_pad,), F32).at[:lora].set(shift_mu[3 * rw_width:])
    l1 = decay_lora
    l2 = l1 + iclr_lora
    wd_pad = jnp.zeros((lora_pad, rw_width), F32).at[:l1].set(w_decay_up)
    wa_pad = jnp.zeros((lora_pad, rw_width), F32).at[l1:l2].set(w_iclr_up)
    wg_pad = jnp.zeros((lora_pad, rw_width), F32).at[l2:lora].set(w_gate_up)
    head = jnp.arange(rw_width) // HEAD_DIM
    hsum = (head[:, None] == head[None, :]).astype(BF16)
    q_c, yv_c, bonus, g, m_c, n_c = _rw_chunk(rkv, lora_cols, mu_rkv, mu_lora, w0, wd_pad, a0, wa_pad,
                                              wg_pad, k_k, k_a, r_k.reshape(-1), hsum, seq)
    y_rw = _rw_state(q_c, yv_c, bonus, g, m_c, n_c, lnx_w, lnx_b, hsum, batch, seq)

    wrt = jnp.zeros((d, LANES), F32).at[:, :n_groups].set(w_group)
    wrt = wrt.at[:, n_groups:n_groups + n_experts].set(w_router)
    wrt = jnp.concatenate(_split(wrt), axis=1)
    x1, h2_tiles, route = _merge(x2, y_sb, y_rw, gate_logits, gate_bias, w_o_sb.astype(BF16),
                                 w_o_rw.astype(BF16), w_out.astype(BF16), gains[1], gt1, gains[2],
                                 sc2, sh2, wrt, seq, n_groups, per_group)

    tile_expert, tile_valid, row_tok, pos = _plan_rows(route, n_experts, TM_EXP)
    ye = _experts(h2_tiles, tile_expert, tile_valid, row_tok,
                  w_up1, w_up3, w_down)
    out = _combine(ye, pos, x1, route, gains[3], gt2, seq)
    return out.reshape(batch, seq, d)


def kernel(x, c, w_mod, b_mod, norm_gains, w_in, gate_bias, shift_mu, w0, w_decay_up, a0, w_iclr_up,
           w_gate_up, k_k, k_a, r_k, lnx_w, lnx_b, w_o_sb, w_o_rw, w_out, w_group, w_router,
           w_up1, w_up3, w_down):
    for l in range(w_mod.shape[0]):
        x = _layer(x, c, w_mod[l], b_mod[l], norm_gains[l], w_in[l], gate_bias[l], shift_mu[l], w0[l],
                   w_decay_up[l], a0[l], w_iclr_up[l], w_gate_up[l], k_k[l], k_a[l], r_k[l], lnx_w[l],
                   lnx_b[l], w_o_sb[l], w_o_rw[l], w_out[l], w_group[l], w_router[l], w_up1[l],
                   w_up3[l], w_down[l])
    return x
```

```python
import functools
import math

import jax
import jax.numpy as jnp
from jax import lax
from jax.experimental import pallas as pl
from jax.experimental.pallas import tpu as pltpu

F32 = jnp.float32
BF16 = jnp.bfloat16
HIGHEST = lax.Precision.HIGHEST

HEAD_DIM = 64
TOP_K_IN_GROUP = 2
NORM_EPS = 1e-6
GN_EPS = 64e-5
L2_EPS = 1e-12

LANES = 128
SUBLANES = 8
VMEM_LIMIT_BYTES = 56 * 1024 * 1024

TM_PROJ = 512
T_ATT = 256
ATT_DEAD_LOG2 = -150.0
C_SCAN = 64
TM_CHUNK = 256
TM_MERGE = 512
TM_EXP = 256
TM_COMB = 256
GATHER_UNROLL = 8
GATHER_AHEAD = 2
GATHER_BUFS = GATHER_AHEAD + 1

_NN = ((1,), (0,))
_NT = ((1,), (1,))
_TN = ((0,), (0,))


def _params(*sem):
    return pltpu.CompilerParams(dimension_semantics=sem, vmem_limit_bytes=VMEM_LIMIT_BYTES)


def _hdot(a, b):
    return jnp.dot(a, b, precision=HIGHEST, preferred_element_type=F32)


def _split(x):
    hi = x.astype(BF16)
    return hi, (x - hi.astype(F32)).astype(BF16)


def _mm1(a, b, dims=_NN):
    return lax.dot_general(a.astype(BF16), b.astype(BF16), (dims, ((), ())), preferred_element_type=F32)


def _mm01(a, b, exact):
    if exact == "a":
        hi, lo = _split(b)
        return jnp.dot(a, hi, preferred_element_type=F32) + jnp.dot(a, lo, preferred_element_type=F32)
    hi, lo = _split(a)
    return jnp.dot(hi, b, preferred_element_type=F32) + jnp.dot(lo, b, preferred_element_type=F32)


def _rms(x):
    return x * lax.rsqrt(jnp.mean(x * x, axis=-1, keepdims=True) + NORM_EPS)


def _sigmoid(x):
    return 1.0 / (1.0 + jnp.exp(-x))


def _mod_kernel(c_ref, w_ref, b_ref, o_ref):
    c = c_ref[...]
    o_ref[...] = _hdot(c * _sigmoid(c), w_ref[...]) + b_ref[...]


def _modulation(c, w_mod, b_mod):
    b, d = c.shape
    n_out = w_mod.shape[1]
    rows = -(-b // SUBLANES) * SUBLANES
    c_pad = jnp.zeros((rows, d), F32).at[:b].set(c)
    out = pl.pallas_call(
        _mod_kernel,
        grid=(n_out // d,),
        in_specs=[pl.BlockSpec((rows, d), lambda j: (0, 0)),
                  pl.BlockSpec((d, d), lambda j: (0, j)),
                  pl.BlockSpec((1, d), lambda j: (0, j))],
        out_specs=pl.BlockSpec((rows, d), lambda j: (0, j)),
        out_shape=jax.ShapeDtypeStruct((rows, n_out), F32),
        compiler_params=_params("parallel"),
        name="mod",
    )(c_pad, w_mod, b_mod.reshape(1, n_out))
    return out[:b]


def _inproj_kernel(x_ref, g_ref, sc_ref, sh_ref, wq_ref, wr_ref, wl_ref, wg_ref,
                   qkv_ref, rkv_ref, lora_ref, gate_ref):
    h = _rms(x_ref[...]) * g_ref[...] * (1.0 + sc_ref[...]) + sh_ref[...]
    hb = h.astype(BF16)
    qkv_ref[...] = jnp.dot(hb, wq_ref[...], preferred_element_type=F32).astype(qkv_ref.dtype)
    rkv_ref[...] = jnp.dot(hb, wr_ref[...], preferred_element_type=F32)
    lora_ref[...] = jnp.dot(hb, wl_ref[...], preferred_element_type=F32)
    gate_ref[...] = jnp.dot(hb, wg_ref[...], preferred_element_type=F32)


def _const_spec(shape):
    return pl.BlockSpec(shape, lambda *_: (0,) * len(shape))


def _batch_spec(d, tiles_per_batch):
    return pl.BlockSpec((None, 1, d), lambda i: (i // tiles_per_batch, 0, 0))


def _inproj(x2, gain, sc, sh, wq, wr, wl, wg, seq):
    n, d = x2.shape
    tm = min(TM_PROJ, seq)
    tpb = seq // tm
    outs = [(wq.shape[1], BF16), (wr.shape[1], F32), (wl.shape[1], F32), (wg.shape[1], F32)]
    return pl.pallas_call(
        _inproj_kernel,
        grid=(n // tm,),
        in_specs=[pl.BlockSpec((tm, d), lambda i: (i, 0)),
                  _const_spec((1, d)), _batch_spec(d, tpb), _batch_spec(d, tpb),
                  _const_spec(wq.shape), _const_spec(wr.shape), _const_spec(wl.shape),
                  _const_spec(wg.shape)],
        out_specs=[pl.BlockSpec((tm, w), lambda i: (i, 0)) for w, _ in outs],
        out_shape=[jax.ShapeDtypeStruct((n, w), dt) for w, dt in outs],
        compiler_params=_params("parallel"),
        name="inproj",
    )(x2, gain, sc, sh, wq, wr, wl, wg)


def _sb_attn_kernel(q_ref, k_ref, v_ref, o_ref, acc_ref, *, scale):
    t = q_ref.shape[0]
    qi = pl.program_id(2)
    row = lax.broadcasted_iota(jnp.int32, (t, t), 0)
    col = lax.broadcasted_iota(jnp.int32, (t, t), 1)
    tri = jnp.where(row > col, 1.0, 0.0).astype(BF16)
    strict = col < row
    lane = lax.broadcasted_iota(jnp.int32, (t, LANES), 1)
    first = lane < HEAD_DIM
    q2 = q_ref[...].astype(F32) * (scale * math.log2(math.e))
    qs = (jnp.where(first, q2, 0.0).astype(BF16), jnp.where(first, 0.0, q2).astype(BF16))

    def blocks(kbs, runs, masked):
        k2 = [k_ref[pl.ds(pl.multiple_of(kb * t, t), t), :] for kb in kbs]
        v2 = [v_ref[pl.ds(pl.multiple_of(kb * t, t), t), :] for kb in kbs]
        z = [[lax.dot_general(qs[hh], k2[j], (_NT, ((), ())), preferred_element_type=F32) for hh in range(2)]
             for j in range(len(kbs))]
        runs = list(runs)
        for j in range(len(kbs)):
            for hh in range(2):
                zz = z[j][hh]
                neg_part = jnp.minimum(zz, 0.0)
                neg_other = neg_part - zz
                soft = jnp.log2(1.0 + jnp.exp2(neg_part + neg_other))
                log_beta = neg_part - soft
                log_keep = neg_other - soft
                if masked[j]:
                    log_keep = jnp.where(strict, log_keep, 0.0)
                later = jnp.dot(log_keep.astype(BF16), tri, preferred_element_type=F32)
                att = jnp.exp2(log_beta + later + runs[hh])
                if masked[j]:
                    att = jnp.where(strict, att, 0.0)
                acc_ref[hh] += jnp.dot(att.astype(BF16), v2[j], preferred_element_type=F32)
                runs[hh] = runs[hh] + jnp.sum(log_keep, axis=-1, keepdims=True)
        return tuple(runs)

    acc_ref[...] = jnp.zeros_like(acc_ref)
    zero = jnp.zeros((t, 1), F32)
    runs = lax.cond(qi > 0,
                    lambda: blocks([qi, qi - 1], (zero, zero), (True, False)),
                    lambda: blocks([qi], (zero, zero), (True,)))

    def live(state):
        kb, r = state
        return jnp.logical_and(kb >= 0, jnp.max(jnp.maximum(r[0], r[1])) > ATT_DEAD_LOG2)

    def body(state):
        kb, r = state
        return kb - 1, blocks([kb], r, (False,))

    lax.while_loop(live, body, (qi - 2, runs))
    o_ref[...] = jnp.where(first, acc_ref[0], acc_ref[1]).astype(o_ref.dtype)


def _sb_attention(qkv, batch, seq, width):
    n = qkv.shape[0]
    t = min(T_ATT, seq)
    n_pairs = width // LANES
    nq = seq // t
    kernel = functools.partial(_sb_attn_kernel, scale=1.0 / math.sqrt(HEAD_DIM))
    return pl.pallas_call(
        kernel,
        grid=(batch, n_pairs, nq),
        in_specs=[pl.BlockSpec((t, LANES), lambda b, p, i: (b * nq + i, p)),
                  pl.BlockSpec((seq, LANES), lambda b, p, i: (b, n_pairs + p)),
                  pl.BlockSpec((seq, LANES), lambda b, p, i: (b, 2 * n_pairs + p))],
        out_specs=pl.BlockSpec((t, LANES), lambda b, p, i: (b * nq + i, p)),
        out_shape=jax.ShapeDtypeStruct((n, width), BF16),
        scratch_shapes=[pltpu.VMEM((2, t, LANES), F32)],
        compiler_params=_params("parallel", "parallel", "parallel"),
        name="sb_attn",
    )(qkv, qkv, qkv)


def _shift_rows(p, halo_row):
    rolled = pltpu.roll(p, 1, 0)
    row = lax.broadcasted_iota(jnp.int32, p.shape, 0)
    return jnp.where(row == 0, halo_row, rolled)


def _rw_chunk_kernel(rkv_ref, rkv_halo_ref, lora_ref, lora_halo_ref, mu_rkv_ref, mu_lora_ref,
                     w0_ref, wd_ref, a0_ref, wa_ref, wg_ref, kk_ref, ka_ref, rk_ref, hsum_ref,
                     q_out, yv_out, bonus_out, g_out, m_out, n_out, *, tiles_per_batch, chunk):
    i = pl.program_id(0)
    keep = jnp.where(i % tiles_per_batch == 0, 0.0, 1.0)
    w = w0_ref.shape[1]
    tm = rkv_ref.shape[0]
    c = chunk
    hd = HEAD_DIM
    n_heads = w // hd

    p = rkv_ref[...]
    prev = _shift_rows(p, rkv_halo_ref[SUBLANES - 1:SUBLANES, :] * keep)
    xs = p + mu_rkv_ref[...] * (prev - p)
    pl_ = lora_ref[...]
    prevl = _shift_rows(pl_, lora_halo_ref[SUBLANES - 1:SUBLANES, :] * keep)
    xl = pl_ + mu_lora_ref[...] * (prevl - pl_)

    r = xs[:, :w]
    k = xs[:, w:2 * w]
    v = xs[:, 2 * w:]
    dec_in = w0_ref[...] + _mm1(jnp.tanh(xl), wd_ref[...])
    neg = -dec_in
    softplus = jnp.maximum(neg, 0.0) + jnp.log(1.0 + jnp.exp(-jnp.abs(neg)))
    lw = -jnp.exp(-softplus - 0.5)
    a_gate = _sigmoid(a0_ref[...] + _mm1(xl, wa_ref[...]))
    g_out[...] = _mm1(_sigmoid(xl), wg_ref[...])
    kk = k * kk_ref[...]
    k = k * (1.0 + (a_gate - 1.0) * ka_ref[...])
    hsum = hsum_ref[...]
    kkn = kk / jnp.maximum(jnp.sqrt(_mm01(kk * kk, hsum, "b")), L2_EPS)
    a = -kkn
    b = kkn * a_gate
    bonus_out[...] = _mm01(r * k * rk_ref[...], hsum, "b") * v

    row = lax.broadcasted_iota(jnp.int32, (c, c), 0)
    col = lax.broadcasted_iota(jnp.int32, (c, c), 1)
    incl = col <= row
    strict = col < row
    tril = jnp.where(incl, 1.0, 0.0).astype(BF16)
    eye = jnp.where(row == col, 1.0, 0.0)

    a_t, r_t, b_t, k_t, b_h, k_h, v_p, g_c = [], [], [], [], [], [], [], []
    for g in range(tm // c):
        rs = slice(g * c, (g + 1) * c)
        lw_c = lw[rs]
        cum = _mm01(tril, lw_c, "a")
        total = cum[c - 1:c, :]
        inv = jnp.exp(-cum)
        tail = jnp.exp(total - cum)
        full = {"a_t": a[rs] * jnp.exp(cum - lw_c), "r_t": r[rs] * jnp.exp(cum), "b_t": b[rs] * inv,
                "k_t": k[rs] * inv, "b_h": b[rs] * tail, "k_h": k[rs] * tail, "v": v[rs],
                "g_c": jnp.exp(total)}
        for h in range(n_heads):
            sl = slice(h * hd, (h + 1) * hd)
            a_t.append(full["a_t"][:, sl]); r_t.append(full["r_t"][:, sl])
            b_t.append(full["b_t"][:, sl]); k_t.append(full["k_t"][:, sl])
            b_h.append(full["b_h"][:, sl]); k_h.append(full["k_h"][:, sl])
            v_p.append(full["v"][:, sl]); g_c.append(full["g_c"][:, sl])
    n_prob = len(a_t)
    prob = range(n_prob)

    lhs = [jnp.concatenate([a_t[j], r_t[j]], axis=0) for j in prob]
    gram_b = [_mm1(lhs[j], b_t[j], _NT) for j in prob]
    gram_k = [_mm1(lhs[j], k_t[j], _NT) for j in prob]
    a_ab = [jnp.where(strict, gram_b[j][:c], 0.0) for j in prob]
    a_rb = [jnp.where(incl, gram_b[j][c:], 0.0) for j in prob]
    a_ak = [jnp.where(strict, gram_k[j][:c], 0.0) for j in prob]
    a_rk = [jnp.where(incl, gram_k[j][c:], 0.0) for j in prob]
    ak_v = [_mm1(a_ak[j], v_p[j]) for j in prob]
    rk_v = [_mm1(a_rk[j], v_p[j]) for j in prob]
    wu = [jnp.concatenate([a_t[j], ak_v[j]], axis=1) for j in prob]
    power = a_ab
    n_apply = max(1, math.ceil(math.log2(c)))
    for step in range(n_apply):
        wu = [wu[j] + _mm1(power[j], wu[j]) for j in prob]
        if step + 1 < n_apply:
            power = [_mm1(power[j], power[j]) for j in prob]
    qy = [_mm1(a_rb[j], wu[j]) + jnp.concatenate([r_t[j], rk_v[j]], axis=1) for j in prob]
    wu_b = [_mm1(wu[j], b_h[j], _TN) for j in prob]
    v_k = [_mm1(v_p[j], k_h[j], _TN) for j in prob]
    for j in prob:
        g, h = divmod(j, n_heads)
        rs = slice(g * c, (g + 1) * c)
        sl = slice(h * hd, (h + 1) * hd)
        q_out[rs, sl] = qy[j][:, :hd]
        yv_out[rs, sl] = qy[j][:, hd:]
        m_out[g, h] = wu_b[j][:hd] + eye * g_c[j]
        n_out[g, h] = wu_b[j][hd:] + v_k[j]


def _rw_chunk(rkv, lora, mu_rkv, mu_lora, w0, wd_pad, a0, wa_pad, wg_pad, k_k, k_a, r_k, hsum, seq):
    n, w3 = rkv.shape
    w = w3 // 3
    wl = lora.shape[1]
    c = min(C_SCAN, seq)
    tm = min(TM_CHUNK, seq)
    tpb = seq // tm
    cpt = tm // c
    hb = tm // SUBLANES
    n_heads = w // HEAD_DIM
    kernel = functools.partial(_rw_chunk_kernel, tiles_per_batch=tpb, chunk=c)
    row = lambda i: (i, 0)
    halo = lambda i: (jnp.maximum(i * hb - 1, 0), 0)
    vec = lambda a: a.reshape(1, -1)
    mat = pl.BlockSpec((cpt, n_heads, HEAD_DIM, HEAD_DIM), lambda i: (i, 0, 0, 0))
    mat_shape = jax.ShapeDtypeStruct((n // c, n_heads, HEAD_DIM, HEAD_DIM), F32)
    return pl.pallas_call(
        kernel,
        grid=(n // tm,),
        in_specs=[pl.BlockSpec((tm, w3), row), pl.BlockSpec((SUBLANES, w3), halo),
                  pl.BlockSpec((tm, wl), row), pl.BlockSpec((SUBLANES, wl), halo),
                  _const_spec((1, w3)), _const_spec((1, wl)),
                  _const_spec((1, w)), _const_spec((wl, w)), _const_spec((1, w)),
                  _const_spec((wl, w)), _const_spec((wl, w)),
                  _const_spec((1, w)), _const_spec((1, w)), _const_spec((1, w)),
                  _const_spec((w, w))],
        out_specs=[pl.BlockSpec((tm, w), row)] * 4 + [mat, mat],
        out_shape=[jax.ShapeDtypeStruct((n, w), F32)] * 4 + [mat_shape, mat_shape],
        compiler_params=_params("parallel"),
        name="rw_chunk",
    )(rkv, rkv, lora, lora, vec(mu_rkv), vec(mu_lora), vec(w0), wd_pad, vec(a0), wa_pad, wg_pad,
      vec(k_k), vec(k_a), vec(r_k), hsum)


def _rw_state_kernel(q_ref, yv_ref, bonus_ref, g_ref, m_ref, n_ref, lnw_ref, lnb_ref, hsum_ref,
                     o_ref, state_ref):
    n_batch, c, w = q_ref.shape
    hd = HEAD_DIM
    n_heads = w // hd

    @pl.when(pl.program_id(0) == 0)
    def _():
        state_ref[...] = jnp.zeros_like(state_ref)

    pairs = [(bi, h) for bi in range(n_batch) for h in range(n_heads)]
    state = [state_ref[bi, h] for bi, h in pairs]
    y = [_mm1(q_ref[bi, :, h * hd:(h + 1) * hd], state[j], _NT) + yv_ref[bi, :, h * hd:(h + 1) * hd]
         for j, (bi, h) in enumerate(pairs)]
    new_state = [_mm1(state[j], m_ref[bi, h]) + n_ref[bi, h] for j, (bi, h) in enumerate(pairs)]
    for j, (bi, h) in enumerate(pairs):
        state_ref[bi, h] = new_state[j]

    hsum = hsum_ref[...]
    y_all = jnp.concatenate([jnp.concatenate(y[bi * n_heads:(bi + 1) * n_heads], axis=1)
                             for bi in range(n_batch)], axis=0)
    mean = _mm01(y_all, hsum, "b") * (1.0 / hd)
    dev = y_all - mean
    var = _mm01(dev * dev, hsum, "b") * (1.0 / hd)
    yn = dev * lax.rsqrt(var + GN_EPS) * lnw_ref[...] + lnb_ref[...]
    for bi in range(n_batch):
        rows = slice(bi * c, (bi + 1) * c)
        o_ref[bi] = ((yn[rows] + bonus_ref[bi]) * g_ref[bi]).astype(o_ref.dtype)


def _rw_state(q, yv, bonus, g, m, n_mat, lnx_w, lnx_b, hsum, batch, seq):
    w = q.shape[1]
    c = min(C_SCAN, seq)
    nc = seq // c
    n_heads = w // HEAD_DIM
    rows = pl.BlockSpec((batch, c, w), lambda ci: (0, ci, 0))
    mats = pl.BlockSpec((batch, None, n_heads, HEAD_DIM, HEAD_DIM), lambda ci: (0, ci, 0, 0, 0))
    to3 = lambda a: a.reshape(batch, seq, w)
    to5 = lambda a: a.reshape(batch, nc, n_heads, HEAD_DIM, HEAD_DIM)
    out = pl.pallas_call(
        _rw_state_kernel,
        grid=(nc,),
        in_specs=[rows] * 4 + [mats, mats, _const_spec((1, w)), _const_spec((1, w)), _const_spec((w, w))],
        out_specs=rows,
        out_shape=jax.ShapeDtypeStruct((batch, seq, w), BF16),
        scratch_shapes=[pltpu.VMEM((batch, n_heads, HEAD_DIM, HEAD_DIM), F32)],
        compiler_params=_params("arbitrary"),
        name="rw_state",
    )(to3(q), to3(yv), to3(bonus), to3(g), to5(m), to5(n_mat), lnx_w.reshape(1, w), lnx_b.reshape(1, w), hsum)
    return out.reshape(batch * seq, w)


def _lane_min_index(mask, lane):
    return jnp.min(jnp.where(mask, lane, 4 * LANES), axis=-1, keepdims=True)


def _merge_kernel(x_ref, ysb_ref, yrw_ref, gl_ref, gb_ref, wsb_ref, wrw_ref, wout_ref,
                  g1_ref, gt1_ref, g2_ref, sc2_ref, sh2_ref, wrt_ref,
                  x1_ref, h2_ref, route_ref, *, n_groups, per_group):
    d = x_ref.shape[1]
    tm = x_ref.shape[0]
    gates = _sigmoid(gl_ref[...] + gb_ref[...])
    merged = (gates[:, :d] * jnp.dot(ysb_ref[...], wsb_ref[...], preferred_element_type=F32)
              + gates[:, d:] * jnp.dot(yrw_ref[...], wrw_ref[...], preferred_element_type=F32))
    y = jnp.dot(merged.astype(BF16), wout_ref[...], preferred_element_type=F32)
    x1 = x_ref[...] + gt1_ref[...] * (_rms(y) * g1_ref[...])
    x1_ref[...] = x1
    h2 = _rms(x1) * g2_ref[...] * (1.0 + sc2_ref[...]) + sh2_ref[...]
    for j in range(d // LANES):
        h2_ref[pl.ds(j, tm, stride=SUBLANES), :] = h2[:, j * LANES:(j + 1) * LANES]

    h_hi, h_lo = _split(h2)
    hi_part = jnp.dot(h_hi, wrt_ref[...], preferred_element_type=F32)
    lg = hi_part[:, :LANES] + (hi_part[:, LANES:] + jnp.dot(h_lo, wrt_ref[:, :LANES], preferred_element_type=F32))
    lane = lax.broadcasted_iota(jnp.int32, lg.shape, 1)
    neg_inf = -jnp.inf
    gmask = lane < n_groups
    gmax = jnp.max(jnp.where(gmask, lg, neg_inf), axis=-1, keepdims=True)
    gsel = _lane_min_index(gmask & (lg == gmax), lane)
    gsum = jnp.sum(jnp.where(gmask, jnp.exp(lg - gmax), 0.0), axis=-1, keepdims=True)
    group_w = 1.0 / gsum
    lo = n_groups + gsel * per_group
    emask = (lane >= lo) & (lane < lo + per_group)
    m1 = jnp.max(jnp.where(emask, lg, neg_inf), axis=-1, keepdims=True)
    i1 = _lane_min_index(emask & (lg == m1), lane)
    rest = emask & (lane != i1)
    m2 = jnp.max(jnp.where(rest, lg, neg_inf), axis=-1, keepdims=True)
    i2 = _lane_min_index(rest & (lg == m2), lane)
    psum = jnp.sum(jnp.where(emask, jnp.exp(lg - m1), 0.0), axis=-1, keepdims=True)
    p1 = 1.0 / psum
    p2 = jnp.exp(m2 - m1) / psum
    w1 = group_w * (p1 / (p1 + p2))
    w2 = group_w * (p2 / (p1 + p2))
    e1 = (i1 - n_groups).astype(F32)
    e2 = (i2 - n_groups).astype(F32)
    route_ref[...] = jnp.where(lane == 0, e1, jnp.where(lane == 1, e2,
                               jnp.where(lane == 2, w1, jnp.where(lane == 3, w2, 0.0))))


def _merge(x2, ysb, yrw, gate_logits, gate_bias, wsb, wrw, wout, g1, gt1, g2, sc2, sh2, wrt,
           seq, n_groups, per_group):
    n, d = x2.shape
    tm = min(TM_MERGE, seq)
    tpb = seq // tm
    w = ysb.shape[1]
    row = lambda i: (i, 0)
    kernel = functools.partial(_merge_kernel, n_groups=n_groups, per_group=per_group)
    return pl.pallas_call(
        kernel,
        grid=(n // tm,),
        in_specs=[pl.BlockSpec((tm, d), row), pl.BlockSpec((tm, w), row), pl.BlockSpec((tm, w), row),
                  pl.BlockSpec((tm, 2 * d), row), _const_spec((1, 2 * d)),
                  _const_spec(wsb.shape), _const_spec(wrw.shape), _const_spec(wout.shape),
                  _const_spec((1, d)), _batch_spec(d, tpb), _const_spec((1, d)),
                  _batch_spec(d, tpb), _batch_spec(d, tpb), _const_spec(wrt.shape)],
        out_specs=[pl.BlockSpec((tm, d), row),
                   pl.BlockSpec((tm * SUBLANES, LANES), row),
                   pl.BlockSpec((tm, LANES), row)],
        out_shape=[jax.ShapeDtypeStruct((n, d), F32),
                   jax.ShapeDtypeStruct((n * SUBLANES, LANES), F32),
                   jax.ShapeDtypeStruct((n, LANES), F32)],
        compiler_params=_params("parallel"),
        name="merge",
    )(x2, ysb, yrw, gate_logits, gate_bias.reshape(1, -1), wsb, wrw, wout, g1, gt1, g2, sc2, sh2, wrt)


def _tile_rows(ref, index):
    return ref.at[pl.ds(pl.multiple_of(index * SUBLANES, SUBLANES), SUBLANES), :]


def _row_gather(src_hbm, idx_ref, base, dst, sem, n_rows):
    def body(r8, _):
        for u in range(GATHER_UNROLL):
            r = r8 * GATHER_UNROLL + u
            pltpu.make_async_copy(_tile_rows(src_hbm, idx_ref[base + r]), _tile_rows(dst, r), sem).start()
        return 0
    lax.fori_loop(0, n_rows // GATHER_UNROLL, body, 0)


def _row_gather_inline(src_hbm, idx_ref, base, dst, sem, n_rows, priorities=(0, 1)):
    for r in range(n_rows):
        pltpu.make_async_copy(_tile_rows(src_hbm, idx_ref[base + r]),
                              dst.at[pl.ds(r * SUBLANES, SUBLANES), :], sem).start(
                                  priority=priorities[r % len(priorities)])


def _row_gather_wait(src_hbm, dst, sem):
    pltpu.make_async_copy(src_hbm.at[pl.ds(0, dst.shape[0]), :], dst, sem).wait()


def _expert_kernel(tile_expert_ref, tile_valid_ref, row_tok_ref,
                   h2_hbm, w1_ref, w3_ref, wd_ref,
                   out_ref, gbuf_0, gbuf_1, gbuf_2, sems, w1_bf, w3_bf, wd_bf):
    tm = out_ref.shape[0] // SUBLANES
    d = w1_ref.shape[0]
    j = pl.program_id(0)
    phase = j % GATHER_BUFS
    bufs = (gbuf_0, gbuf_1, gbuf_2)
    valid = tile_valid_ref[j] == 1
    prev_valid = tile_valid_ref[jnp.maximum(j - 1, 0)] == 1

    @pl.when(jnp.logical_or(j == 0, tile_expert_ref[j] != tile_expert_ref[jnp.maximum(j - 1, 0)]))
    def _():
        w1_bf[...] = w1_ref[...].astype(BF16)
        w3_bf[...] = w3_ref[...].astype(BF16)
        wd_bf[...] = wd_ref[...].astype(BF16)

    @pl.when(j == 0)
    def _():
        for t in range(GATHER_AHEAD):
            _row_gather(h2_hbm, row_tok_ref, t * tm, bufs[t], sems.at[t], tm)

    def tile(p):
        cur = bufs[p]
        ahead = (p + GATHER_AHEAD) % GATHER_BUFS
        _row_gather_wait(h2_hbm, cur, sems.at[p])
        _row_gather_inline(h2_hbm, row_tok_ref, (j + GATHER_AHEAD) * tm, bufs[ahead], sems.at[ahead], tm,
                           priorities=(1,))
        x = jnp.concatenate([cur[pl.ds(c, tm, stride=SUBLANES), :].astype(BF16)
                             for c in range(d // LANES)], axis=1)
        up1 = jnp.dot(x, w1_bf[...], preferred_element_type=F32)
        up3 = jnp.dot(x, w3_bf[...], preferred_element_type=F32)
        hid = up1 * _sigmoid(up1) * up3
        y = jnp.dot(hid.astype(BF16), wd_bf[...], preferred_element_type=F32)
        for c in range(d // LANES):
            out_ref[pl.ds(c, tm, stride=SUBLANES), :] = y[:, c * LANES:(c + 1) * LANES]

    for p in range(GATHER_BUFS):
        pl.when(jnp.logical_and(valid, phase == p))(functools.partial(tile, p))

    @pl.when(jnp.logical_not(valid))
    def _():
        drain = jnp.logical_and(j > 0, prev_valid)
        for p in range(GATHER_BUFS):
            @pl.when(jnp.logical_and(drain, phase == p))
            def _():
                for t in range(GATHER_AHEAD):
                    q = (p + t) % GATHER_BUFS
                    _row_gather_wait(h2_hbm, bufs[q], sems.at[q])

        out_ref[...] = jnp.zeros_like(out_ref)


def _experts(h2_tiles, tile_expert, tile_valid, row_tok, w1, w3, wd):
    n_tiles = tile_expert.shape[0]
    tm = TM_EXP
    _, d, de = w1.shape
    rows = n_tiles * tm
    grid_spec = pltpu.PrefetchScalarGridSpec(
        num_scalar_prefetch=3,
        grid=(n_tiles,),
        in_specs=[pl.BlockSpec(memory_space=pl.ANY),
                  pl.BlockSpec((None, d, de), lambda j, te, tv, rt: (te[j], 0, 0)),
                  pl.BlockSpec((None, d, de), lambda j, te, tv, rt: (te[j], 0, 0)),
                  pl.BlockSpec((None, de, d), lambda j, te, tv, rt: (te[j], 0, 0))],
        out_specs=pl.BlockSpec((tm * SUBLANES, LANES), lambda j, te, tv, rt: (j, 0)),
        scratch_shapes=[pltpu.VMEM((tm * SUBLANES, LANES), F32) for _ in range(GATHER_BUFS)]
                       + [pltpu.SemaphoreType.DMA((GATHER_BUFS,)),
                        pltpu.VMEM((d, de), BF16), pltpu.VMEM((d, de), BF16), pltpu.VMEM((de, d), BF16)],
    )
    return pl.pallas_call(
        _expert_kernel,
        grid_spec=grid_spec,
        out_shape=jax.ShapeDtypeStruct((rows * SUBLANES, LANES), F32),
        compiler_params=_params("arbitrary"),
        name="experts",
    )(tile_expert, tile_valid, row_tok, h2_tiles, w1, w3, wd)


def _combine_kernel(pos_ref,
                    ye_hbm, x1_ref, route_ref, g3_ref, gt2_ref,
                    out_ref, gbuf_0, gbuf_1, gbuf_2, sems, *, n_tokens):
    tm, d = x1_ref.shape
    i = pl.program_id(0)
    nt = pl.num_programs(0)
    phase = i % GATHER_BUFS
    bufs = (gbuf_0, gbuf_1, gbuf_2)

    def fetch(tile_index, buf, sem, issue):
        for k in range(TOP_K_IN_GROUP):
            issue(ye_hbm, pos_ref, k * n_tokens + tile_index * tm, buf.at[k], sem, tm)

    @pl.when(i == 0)
    def _():
        for t in range(GATHER_AHEAD):
            fetch(jnp.minimum(t, nt - 1), bufs[t], sems.at[t], _row_gather)

    def tile(p):
        cur = bufs[p]
        for k in range(TOP_K_IN_GROUP):
            _row_gather_wait(ye_hbm, cur.at[k], sems.at[p])
        ahead = (p + GATHER_AHEAD) % GATHER_BUFS
        fetch(jnp.minimum(i + GATHER_AHEAD, nt - 1), bufs[ahead], sems.at[ahead], _row_gather_inline)
        wts = [route_ref[:, TOP_K_IN_GROUP + k:TOP_K_IN_GROUP + k + 1] for k in range(TOP_K_IN_GROUP)]
        y = jnp.concatenate(
            [sum(wts[k] * cur[k, pl.ds(c, tm, stride=SUBLANES), :] for k in range(TOP_K_IN_GROUP))
             for c in range(d // LANES)], axis=1)
        out_ref[...] = x1_ref[...] + gt2_ref[...] * (_rms(y) * g3_ref[...])

        @pl.when(i == nt - 1)
        def _():
            for t in range(1, GATHER_BUFS):
                q = (p + t) % GATHER_BUFS
                for k in range(TOP_K_IN_GROUP):
                    _row_gather_wait(ye_hbm, bufs[q].at[k], sems.at[q])

    for p in range(GATHER_BUFS):
        pl.when(phase == p)(functools.partial(tile, p))


def _combine(ye_tiles, pos, x1, route, g3, gt2, seq):
    n, d = x1.shape
    tm = min(TM_COMB, seq)
    tpb = seq // tm
    kernel = functools.partial(_combine_kernel, n_tokens=n)
    grid_spec = pltpu.PrefetchScalarGridSpec(
        num_scalar_prefetch=1,
        grid=(n // tm,),
        in_specs=[pl.BlockSpec(memory_space=pl.ANY),
                  pl.BlockSpec((tm, d), lambda i, p: (i, 0)),
                  pl.BlockSpec((tm, LANES), lambda i, p: (i, 0)),
                  pl.BlockSpec((1, d), lambda i, p: (0, 0)),
                  pl.BlockSpec((None, 1, d), lambda i, p: (i // tpb, 0, 0))],
        out_specs=pl.BlockSpec((tm, d), lambda i, p: (i, 0)),
        scratch_shapes=[pltpu.VMEM((TOP_K_IN_GROUP, tm * SUBLANES, LANES), F32) for _ in range(GATHER_BUFS)]
                       + [pltpu.SemaphoreType.DMA((GATHER_BUFS,))],
    )
    return pl.pallas_call(
        kernel,
        grid_spec=grid_spec,
        out_shape=jax.ShapeDtypeStruct((n, d), F32),
        compiler_params=_params("arbitrary"),
        name="combine",
    )(pos, ye_tiles, x1, route, g3, gt2)


def _plan_rows(route, n_experts, tm):
    n = route.shape[0]
    pairs = TOP_K_IN_GROUP * n
    expert = jnp.concatenate([route[:, k] for k in range(TOP_K_IN_GROUP)]).astype(jnp.int32)
    token = jnp.tile(jnp.arange(n, dtype=jnp.int32), TOP_K_IN_GROUP)
    onehot = (expert[:, None] == jnp.arange(n_experts, dtype=jnp.int32)[None, :]).astype(jnp.int32)
    rank = jnp.sum(jnp.cumsum(onehot, axis=0) * onehot, axis=1) - 1
    counts = jnp.sum(onehot, axis=0)
    padded = -(-counts // tm) * tm
    ends = jnp.cumsum(padded)
    offsets = ends - padded
    pos = jnp.sum(onehot * offsets[None, :], axis=1) + rank
    n_tiles = pairs // tm + n_experts + GATHER_AHEAD
    rows = n_tiles * tm
    row_tok = jnp.zeros((rows,), jnp.int32).at[pos].set(token, unique_indices=True, mode="promise_in_bounds")
    tile_start = jnp.arange(n_tiles, dtype=jnp.int32) * tm
    tile_expert = jnp.minimum(jnp.sum((ends[None, :] <= tile_start[:, None]).astype(jnp.int32), axis=1),
                              n_experts - 1)
    tile_valid = (tile_start < ends[-1]).astype(jnp.int32)
    return tile_expert, tile_valid, row_tok, pos.astype(jnp.int32)


def _layer(x, c, w_mod, b_mod, norm_gains, w_in, gate_bias, shift_mu, w0, w_decay_up, a0, w_iclr_up,
           w_gate_up, k_k, k_a, r_k, lnx_w, lnx_b, w_o_sb, w_o_rw, w_out, w_group, w_router,
           w_up1, w_up3, w_down):
    batch, seq, d = x.shape
    n = batch * seq
    sb_width = w_o_sb.shape[0]
    rw_width = w_o_rw.shape[0]
    decay_lora, iclr_lora, gate_lora = w_decay_up.shape[0], w_iclr_up.shape[0], w_gate_up.shape[0]
    lora = decay_lora + iclr_lora + gate_lora
    lora_pad = -(-lora // LANES) * LANES
    n_groups = w_group.shape[1]
    n_experts = w_router.shape[1]
    per_group = n_experts // n_groups

    mod = _modulation(c, w_mod, b_mod)
    sh1, sc1, gt1, sh2, sc2, gt2 = [m.reshape(batch, 1, d) for m in jnp.split(mod, 6, axis=-1)]
    gains = norm_gains.reshape(4, 1, d)

    o1 = 3 * sb_width
    o2 = o1 + 3 * rw_width
    o3 = o2 + lora
    wq = w_in[:, :o1].astype(BF16)
    wr = w_in[:, o1:o2].astype(BF16)
    wl = jnp.zeros((d, lora_pad), BF16).at[:, :lora].set(w_in[:, o2:o3].astype(BF16))
    wg = w_in[:, o3:].astype(BF16)
    x2 = x.reshape(n, d)
    qkv, rkv, lora_cols, gate_logits = _inproj(x2, gains[0], sc1, sh1, wq, wr, wl, wg, seq)

    y_sb = _sb_attention(qkv, batch, seq, sb_width)

    mu_rkv = shift_mu[:3 * rw_width]
    mu_lora = jnp.zeros((lora_pad,), F32).at[:lora].set(shift_mu[3 * rw_width:])
    l1 = decay_lora
    l2 = l1 + iclr_lora
    wd_pad = jnp.zeros((lora_pad, rw_width), F32).at[:l1].set(w_decay_up)
    wa_pad = jnp.zeros((lora_pad, rw_width), F32).at[l1:l2].set(w_iclr_up)
    wg_pad = jnp.zeros((lora_pad, rw_width), F32).at[l2:lora].set(w_gate_up)
    head = jnp.arange(rw_width) // HEAD_DIM
    hsum = (head[:, None] == head[None, :]).astype(BF16)
    q_c, yv_c, bonus, g, m_c, n_c = _rw_chunk(rkv, lora_cols, mu_rkv, mu_lora, w0, wd_pad, a0, wa_pad,
                                              wg_pad, k_k, k_a, r_k.reshape(-1), hsum, seq)
    y_rw = _rw_state(q_c, yv_c, bonus, g, m_c, n_c, lnx_w, lnx_b, hsum, batch, seq)

    wrt = jnp.zeros((d, LANES), F32).at[:, :n_groups].set(w_group)
    wrt = wrt.at[:, n_groups:n_groups + n_experts].set(w_router)
    wrt = jnp.concatenate(_split(wrt), axis=1)
    x1, h2_tiles, route = _merge(x2, y_sb, y_rw, gate_logits, gate_bias, w_o_sb.astype(BF16),
                                 w_o_rw.astype(BF16), w_out.astype(BF16), gains[1], gt1, gains[2],
                                 sc2, sh2, wrt, seq, n_groups, per_group)

    tile_expert, tile_valid, row_tok, pos = _plan_rows(route, n_experts, TM_EXP)
    ye = _experts(h2_tiles, tile_expert, tile_valid, row_tok,
                  w_up1, w_up3, w_down)
    out = _combine(ye, pos, x1, route, gains[3], gt2, seq)
    return out.reshape(batch, seq, d)


def kernel(x, c, w_mod, b_mod, norm_gains, w_in, gate_bias, shift_mu, w0, w_decay_up, a0, w_iclr_up,
           w_gate_up, k_k, k_a, r_k, lnx_w, lnx_b, w_o_sb, w_o_rw, w_out, w_group, w_router,
           w_up1, w_up3, w_down):
    for l in range(w_mod.shape[0]):
        x = _layer(x, c, w_mod[l], b_mod[l], norm_gains[l], w_in[l], gate_bias[l], shift_mu[l], w0[l],
                   w_decay_up[l], a0[l], w_iclr_up[l], w_gate_up[l], k_k[l], k_a[l], r_k[l], lnx_w[l],
                   lnx_b[l], w_o_sb[l], w_o_rw[l], w_out[l], w_group[l], w_router[l], w_up1[l],
                   w_up3[l], w_down[l])
    return x
```

```python
import functools
import math

import jax
import jax.numpy as jnp
from jax import lax
from jax.experimental import pallas as pl
from jax.experimental.pallas import tpu as pltpu

F32 = jnp.float32
BF16 = jnp.bfloat16
HIGHEST = lax.Precision.HIGHEST

HEAD_DIM = 64
TOP_K_IN_GROUP = 2
NORM_EPS = 1e-6
GN_EPS = 64e-5
L2_EPS = 1e-12

LANES = 128
SUBLANES = 8
VMEM_LIMIT_BYTES = 56 * 1024 * 1024

TM_PROJ = 512
T_ATT = 256
ATT_DEAD_LOG2 = -150.0
C_SCAN = 64
TM_CHUNK = 256
TM_MERGE = 512
TM_EXP = 512
TM_COMB = 256
GATHER_UNROLL = 8
GATHER_AHEAD = 2
GATHER_BUFS = GATHER_AHEAD + 1

_NN = ((1,), (0,))
_NT = ((1,), (1,))
_TN = ((0,), (0,))


def _params(*sem):
    return pltpu.CompilerParams(dimension_semantics=sem, vmem_limit_bytes=VMEM_LIMIT_BYTES)


def _hdot(a, b):
    return jnp.dot(a, b, precision=HIGHEST, preferred_element_type=F32)


def _split(x):
    hi = x.astype(BF16)
    return hi, (x - hi.astype(F32)).astype(BF16)


def _mm1(a, b, dims=_NN):
    return lax.dot_general(a.astype(BF16), b.astype(BF16), (dims, ((), ())), preferred_element_type=F32)


def _mm01(a, b, exact):
    if exact == "a":
        hi, lo = _split(b)
        return jnp.dot(a, hi, preferred_element_type=F32) + jnp.dot(a, lo, preferred_element_type=F32)
    hi, lo = _split(a)
    return jnp.dot(hi, b, preferred_element_type=F32) + jnp.dot(lo, b, preferred_element_type=F32)


def _rms(x):
    return x * lax.rsqrt(jnp.mean(x * x, axis=-1, keepdims=True) + NORM_EPS)


def _sigmoid(x):
    return 1.0 / (1.0 + jnp.exp(-x))


def _mod_kernel(c_ref, w_ref, b_ref, o_ref):
    c = c_ref[...]
    o_ref[...] = _hdot(c * _sigmoid(c), w_ref[...]) + b_ref[...]


def _modulation(c, w_mod, b_mod):
    b, d = c.shape
    n_out = w_mod.shape[1]
    rows = -(-b // SUBLANES) * SUBLANES
    c_pad = jnp.zeros((rows, d), F32).at[:b].set(c)
    out = pl.pallas_call(
        _mod_kernel,
        grid=(n_out // d,),
        in_specs=[pl.BlockSpec((rows, d), lambda j: (0, 0)),
                  pl.BlockSpec((d, d), lambda j: (0, j)),
                  pl.BlockSpec((1, d), lambda j: (0, j))],
        out_specs=pl.BlockSpec((rows, d), lambda j: (0, j)),
        out_shape=jax.ShapeDtypeStruct((rows, n_out), F32),
        compiler_params=_params("parallel"),
        name="mod",
    )(c_pad, w_mod, b_mod.reshape(1, n_out))
    return out[:b]


def _inproj_kernel(x_ref, g_ref, sc_ref, sh_ref, wq_ref, wr_ref, wl_ref, wg_ref,
                   qkv_ref, rkv_ref, lora_ref, gate_ref):
    h = _rms(x_ref[...]) * g_ref[...] * (1.0 + sc_ref[...]) + sh_ref[...]
    hb = h.astype(BF16)
    qkv_ref[...] = jnp.dot(hb, wq_ref[...], preferred_element_type=F32).astype(qkv_ref.dtype)
    rkv_ref[...] = jnp.dot(hb, wr_ref[...], preferred_element_type=F32)
    lora_ref[...] = jnp.dot(hb, wl_ref[...], preferred_element_type=F32)
    gate_ref[...] = jnp.dot(hb, wg_ref[...], preferred_element_type=F32)


def _const_spec(shape):
    return pl.BlockSpec(shape, lambda *_: (0,) * len(shape))


def _batch_spec(d, tiles_per_batch):
    return pl.BlockSpec((None, 1, d), lambda i: (i // tiles_per_batch, 0, 0))


def _inproj(x2, gain, sc, sh, wq, wr, wl, wg, seq):
    n, d = x2.shape
    tm = min(TM_PROJ, seq)
    tpb = seq // tm
    outs = [(wq.shape[1], BF16), (wr.shape[1], F32), (wl.shape[1], F32), (wg.shape[1], F32)]
    return pl.pallas_call(
        _inproj_kernel,
        grid=(n // tm,),
        in_specs=[pl.BlockSpec((tm, d), lambda i: (i, 0)),
                  _const_spec((1, d)), _batch_spec(d, tpb), _batch_spec(d, tpb),
                  _const_spec(wq.shape), _const_spec(wr.shape), _const_spec(wl.shape),
                  _const_spec(wg.shape)],
        out_specs=[pl.BlockSpec((tm, w), lambda i: (i, 0)) for w, _ in outs],
        out_shape=[jax.ShapeDtypeStruct((n, w), dt) for w, dt in outs],
        compiler_params=_params("parallel"),
        name="inproj",
    )(x2, gain, sc, sh, wq, wr, wl, wg)


def _sb_attn_kernel(q_ref, k_ref, v_ref, o_ref, acc_ref, *, scale):
    t = q_ref.shape[0]
    qi = pl.program_id(2)
    row = lax.broadcasted_iota(jnp.int32, (t, t), 0)
    col = lax.broadcasted_iota(jnp.int32, (t, t), 1)
    tri = jnp.where(row > col, 1.0, 0.0).astype(BF16)
    strict = col < row
    lane = lax.broadcasted_iota(jnp.int32, (t, LANES), 1)
    first = lane < HEAD_DIM
    q2 = q_ref[...].astype(F32) * (scale * math.log2(math.e))
    qs = (jnp.where(first, q2, 0.0).astype(BF16), jnp.where(first, 0.0, q2).astype(BF16))

    def blocks(kbs, runs, masked):
        k2 = [k_ref[pl.ds(pl.multiple_of(kb * t, t), t), :] for kb in kbs]
        v2 = [v_ref[pl.ds(pl.multiple_of(kb * t, t), t), :] for kb in kbs]
        z = [[lax.dot_general(qs[hh], k2[j], (_NT, ((), ())), preferred_element_type=F32) for hh in range(2)]
             for j in range(len(kbs))]
        runs = list(runs)
        for j in range(len(kbs)):
            for hh in range(2):
                zz = z[j][hh]
                neg_part = jnp.minimum(zz, 0.0)
                neg_other = neg_part - zz
                soft = jnp.log2(1.0 + jnp.exp2(neg_part + neg_other))
                log_beta = neg_part - soft
                log_keep = neg_other - soft
                if masked[j]:
                    log_keep = jnp.where(strict, log_keep, 0.0)
                later = jnp.dot(log_keep.astype(BF16), tri, preferred_element_type=F32)
                att = jnp.exp2(log_beta + later + runs[hh])
                if masked[j]:
                    att = jnp.where(strict, att, 0.0)
                acc_ref[hh] += jnp.dot(att.astype(BF16), v2[j], preferred_element_type=F32)
                runs[hh] = runs[hh] + jnp.sum(log_keep, axis=-1, keepdims=True)
        return tuple(runs)

    acc_ref[...] = jnp.zeros_like(acc_ref)
    zero = jnp.zeros((t, 1), F32)
    runs = lax.cond(qi > 0,
                    lambda: blocks([qi, qi - 1], (zero, zero), (True, False)),
                    lambda: blocks([qi], (zero, zero), (True,)))

    def live(state):
        kb, r = state
        return jnp.logical_and(kb >= 0, jnp.max(jnp.maximum(r[0], r[1])) > ATT_DEAD_LOG2)

    def body(state):
        kb, r = state
        return kb - 1, blocks([kb], r, (False,))

    lax.while_loop(live, body, (qi - 2, runs))
    o_ref[...] = jnp.where(first, acc_ref[0], acc_ref[1]).astype(o_ref.dtype)


def _sb_attention(qkv, batch, seq, width):
    n = qkv.shape[0]
    t = min(T_ATT, seq)
    n_pairs = width // LANES
    nq = seq // t
    kernel = functools.partial(_sb_attn_kernel, scale=1.0 / math.sqrt(HEAD_DIM))
    return pl.pallas_call(
        kernel,
        grid=(batch, n_pairs, nq),
        in_specs=[pl.BlockSpec((t, LANES), lambda b, p, i: (b * nq + i, p)),
                  pl.BlockSpec((seq, LANES), lambda b, p, i: (b, n_pairs + p)),
                  pl.BlockSpec((seq, LANES), lambda b, p, i: (b, 2 * n_pairs + p))],
        out_specs=pl.BlockSpec((t, LANES), lambda b, p, i: (b * nq + i, p)),
        out_shape=jax.ShapeDtypeStruct((n, width), BF16),
        scratch_shapes=[pltpu.VMEM((2, t, LANES), F32)],
        compiler_params=_params("parallel", "parallel", "parallel"),
        name="sb_attn",
    )(qkv, qkv, qkv)


def _shift_rows(p, halo_row):
    rolled = pltpu.roll(p, 1, 0)
    row = lax.broadcasted_iota(jnp.int32, p.shape, 0)
    return jnp.where(row == 0, halo_row, rolled)


def _rw_chunk_kernel(rkv_ref, rkv_halo_ref, lora_ref, lora_halo_ref, mu_rkv_ref, mu_lora_ref,
                     w0_ref, wd_ref, a0_ref, wa_ref, wg_ref, kk_ref, ka_ref, rk_ref, hsum_ref,
                     q_out, yv_out, bonus_out, g_out, m_out, n_out, *, tiles_per_batch, chunk):
    i = pl.program_id(0)
    keep = jnp.where(i % tiles_per_batch == 0, 0.0, 1.0)
    w = w0_ref.shape[1]
    tm = rkv_ref.shape[0]
    c = chunk
    hd = HEAD_DIM
    n_heads = w // hd

    p = rkv_ref[...]
    prev = _shift_rows(p, rkv_halo_ref[SUBLANES - 1:SUBLANES, :] * keep)
    xs = p + mu_rkv_ref[...] * (prev - p)
    pl_ = lora_ref[...]
    prevl = _shift_rows(pl_, lora_halo_ref[SUBLANES - 1:SUBLANES, :] * keep)
    xl = pl_ + mu_lora_ref[...] * (prevl - pl_)

    r = xs[:, :w]
    k = xs[:, w:2 * w]
    v = xs[:, 2 * w:]
    dec_in = w0_ref[...] + _mm1(jnp.tanh(xl), wd_ref[...])
    neg = -dec_in
    softplus = jnp.maximum(neg, 0.0) + jnp.log(1.0 + jnp.exp(-jnp.abs(neg)))
    lw = -jnp.exp(-softplus - 0.5)
    a_gate = _sigmoid(a0_ref[...] + _mm1(xl, wa_ref[...]))
    g_out[...] = _mm1(_sigmoid(xl), wg_ref[...])
    kk = k * kk_ref[...]
    k = k * (1.0 + (a_gate - 1.0) * ka_ref[...])
    hsum = hsum_ref[...]
    kkn = kk / jnp.maximum(jnp.sqrt(_mm01(kk * kk, hsum, "b")), L2_EPS)
    a = -kkn
    b = kkn * a_gate
    bonus_out[...] = _mm01(r * k * rk_ref[...], hsum, "b") * v

    row = lax.broadcasted_iota(jnp.int32, (c, c), 0)
    col = lax.broadcasted_iota(jnp.int32, (c, c), 1)
    incl = col <= row
    strict = col < row
    tril = jnp.where(incl, 1.0, 0.0).astype(BF16)
    eye = jnp.where(row == col, 1.0, 0.0)

    a_t, r_t, b_t, k_t, b_h, k_h, v_p, g_c = [], [], [], [], [], [], [], []
    for g in range(tm // c):
        rs = slice(g * c, (g + 1) * c)
        lw_c = lw[rs]
        cum = _mm01(tril, lw_c, "a")
        total = cum[c - 1:c, :]
        inv = jnp.exp(-cum)
        tail = jnp.exp(total - cum)
        full = {"a_t": a[rs] * jnp.exp(cum - lw_c), "r_t": r[rs] * jnp.exp(cum), "b_t": b[rs] * inv,
                "k_t": k[rs] * inv, "b_h": b[rs] * tail, "k_h": k[rs] * tail, "v": v[rs],
                "g_c": jnp.exp(total)}
        for h in range(n_heads):
            sl = slice(h * hd, (h + 1) * hd)
            a_t.append(full["a_t"][:, sl]); r_t.append(full["r_t"][:, sl])
            b_t.append(full["b_t"][:, sl]); k_t.append(full["k_t"][:, sl])
            b_h.append(full["b_h"][:, sl]); k_h.append(full["k_h"][:, sl])
            v_p.append(full["v"][:, sl]); g_c.append(full["g_c"][:, sl])
    n_prob = len(a_t)
    prob = range(n_prob)

    lhs = [jnp.concatenate([a_t[j], r_t[j]], axis=0) for j in prob]
    gram_b = [_mm1(lhs[j], b_t[j], _NT) for j in prob]
    gram_k = [_mm1(lhs[j], k_t[j], _NT) for j in prob]
    a_ab = [jnp.where(strict, gram_b[j][:c], 0.0) for j in prob]
    a_rb = [jnp.where(incl, gram_b[j][c:], 0.0) for j in prob]
    a_ak = [jnp.where(strict, gram_k[j][:c], 0.0) for j in prob]
    a_rk = [jnp.where(incl, gram_k[j][c:], 0.0) for j in prob]
    ak_v = [_mm1(a_ak[j], v_p[j]) for j in prob]
    rk_v = [_mm1(a_rk[j], v_p[j]) for j in prob]
    wu = [jnp.concatenate([a_t[j], ak_v[j]], axis=1) for j in prob]
    power = a_ab
    n_apply = max(1, math.ceil(math.log2(c)))
    for step in range(n_apply):
        wu = [wu[j] + _mm1(power[j], wu[j]) for j in prob]
        if step + 1 < n_apply:
            power = [_mm1(power[j], power[j]) for j in prob]
    qy = [_mm1(a_rb[j], wu[j]) + jnp.concatenate([r_t[j], rk_v[j]], axis=1) for j in prob]
    wu_b = [_mm1(wu[j], b_h[j], _TN) for j in prob]
    v_k = [_mm1(v_p[j], k_h[j], _TN) for j in prob]
    for j in prob:
        g, h = divmod(j, n_heads)
        rs = slice(g * c, (g + 1) * c)
        sl = slice(h * hd, (h + 1) * hd)
        q_out[rs, sl] = qy[j][:, :hd]
        yv_out[rs, sl] = qy[j][:, hd:]
        m_out[g, h] = wu_b[j][:hd] + eye * g_c[j]
        n_out[g, h] = wu_b[j][hd:] + v_k[j]


def _rw_chunk(rkv, lora, mu_rkv, mu_lora, w0, wd_pad, a0, wa_pad, wg_pad, k_k, k_a, r_k, hsum, seq):
    n, w3 = rkv.shape
    w = w3 // 3
    wl = lora.shape[1]
    c = min(C_SCAN, seq)
    tm = min(TM_CHUNK, seq)
    tpb = seq // tm
    cpt = tm // c
    hb = tm // SUBLANES
    n_heads = w // HEAD_DIM
    kernel = functools.partial(_rw_chunk_kernel, tiles_per_batch=tpb, chunk=c)
    row = lambda i: (i, 0)
    halo = lambda i: (jnp.maximum(i * hb - 1, 0), 0)
    vec = lambda a: a.reshape(1, -1)
    mat = pl.BlockSpec((cpt, n_heads, HEAD_DIM, HEAD_DIM), lambda i: (i, 0, 0, 0))
    mat_shape = jax.ShapeDtypeStruct((n // c, n_heads, HEAD_DIM, HEAD_DIM), F32)
    return pl.pallas_call(
        kernel,
        grid=(n // tm,),
        in_specs=[pl.BlockSpec((tm, w3), row), pl.BlockSpec((SUBLANES, w3), halo),
                  pl.BlockSpec((tm, wl), row), pl.BlockSpec((SUBLANES, wl), halo),
                  _const_spec((1, w3)), _const_spec((1, wl)),
                  _const_spec((1, w)), _const_spec((wl, w)), _const_spec((1, w)),
                  _const_spec((wl, w)), _const_spec((wl, w)),
                  _const_spec((1, w)), _const_spec((1, w)), _const_spec((1, w)),
                  _const_spec((w, w))],
        out_specs=[pl.BlockSpec((tm, w), row)] * 4 + [mat, mat],
        out_shape=[jax.ShapeDtypeStruct((n, w), F32)] * 4 + [mat_shape, mat_shape],
        compiler_params=_params("parallel"),
        name="rw_chunk",
    )(rkv, rkv, lora, lora, vec(mu_rkv), vec(mu_lora), vec(w0), wd_pad, vec(a0), wa_pad, wg_pad,
      vec(k_k), vec(k_a), vec(r_k), hsum)


def _rw_state_kernel(q_ref, yv_ref, bonus_ref, g_ref, m_ref, n_ref, lnw_ref, lnb_ref, hsum_ref,
                     o_ref, state_ref):
    n_batch, c, w = q_ref.shape
    hd = HEAD_DIM
    n_heads = w // hd

    @pl.when(pl.program_id(0) == 0)
    def _():
        state_ref[...] = jnp.zeros_like(state_ref)

    pairs = [(bi, h) for bi in range(n_batch) for h in range(n_heads)]
    state = [state_ref[bi, h] for bi, h in pairs]
    y = [_mm1(q_ref[bi, :, h * hd:(h + 1) * hd], state[j], _NT) + yv_ref[bi, :, h * hd:(h + 1) * hd]
         for j, (bi, h) in enumerate(pairs)]
    new_state = [_mm1(state[j], m_ref[bi, h]) + n_ref[bi, h] for j, (bi, h) in enumerate(pairs)]
    for j, (bi, h) in enumerate(pairs):
        state_ref[bi, h] = new_state[j]

    hsum = hsum_ref[...]
    y_all = jnp.concatenate([jnp.concatenate(y[bi * n_heads:(bi + 1) * n_heads], axis=1)
                             for bi in range(n_batch)], axis=0)
    mean = _mm01(y_all, hsum, "b") * (1.0 / hd)
    dev = y_all - mean
    var = _mm01(dev * dev, hsum, "b") * (1.0 / hd)
    yn = dev * lax.rsqrt(var + GN_EPS) * lnw_ref[...] + lnb_ref[...]
    for bi in range(n_batch):
        rows = slice(bi * c, (bi + 1) * c)
        o_ref[bi] = ((yn[rows] + bonus_ref[bi]) * g_ref[bi]).astype(o_ref.dtype)


def _rw_state(q, yv, bonus, g, m, n_mat, lnx_w, lnx_b, hsum, batch, seq):
    w = q.shape[1]
    c = min(C_SCAN, seq)
    nc = seq // c
    n_heads = w // HEAD_DIM
    rows = pl.BlockSpec((batch, c, w), lambda ci: (0, ci, 0))
    mats = pl.BlockSpec((batch, None, n_heads, HEAD_DIM, HEAD_DIM), lambda ci: (0, ci, 0, 0, 0))
    to3 = lambda a: a.reshape(batch, seq, w)
    to5 = lambda a: a.reshape(batch, nc, n_heads, HEAD_DIM, HEAD_DIM)
    out = pl.pallas_call(
        _rw_state_kernel,
        grid=(nc,),
        in_specs=[rows] * 4 + [mats, mats, _const_spec((1, w)), _const_spec((1, w)), _const_spec((w, w))],
        out_specs=rows,
        out_shape=jax.ShapeDtypeStruct((batch, seq, w), BF16),
        scratch_shapes=[pltpu.VMEM((batch, n_heads, HEAD_DIM, HEAD_DIM), F32)],
        compiler_params=_params("arbitrary"),
        name="rw_state",
    )(to3(q), to3(yv), to3(bonus), to3(g), to5(m), to5(n_mat), lnx_w.reshape(1, w), lnx_b.reshape(1, w), hsum)
    return out.reshape(batch * seq, w)


def _lane_min_index(mask, lane):
    return jnp.min(jnp.where(mask, lane, 4 * LANES), axis=-1, keepdims=True)


def _merge_kernel(x_ref, ysb_ref, yrw_ref, gl_ref, gb_ref, wsb_ref, wrw_ref, wout_ref,
                  g1_ref, gt1_ref, g2_ref, sc2_ref, sh2_ref, wrt_ref,
                  x1_ref, h2_ref, route_ref, *, n_groups, per_group):
    d = x_ref.shape[1]
    tm = x_ref.shape[0]
    gates = _sigmoid(gl_ref[...] + gb_ref[...])
    merged = (gates[:, :d] * jnp.dot(ysb_ref[...], wsb_ref[...], preferred_element_type=F32)
              + gates[:, d:] * jnp.dot(yrw_ref[...], wrw_ref[...], preferred_element_type=F32))
    y = jnp.dot(merged.astype(BF16), wout_ref[...], preferred_element_type=F32)
    x1 = x_ref[...] + gt1_ref[...] * (_rms(y) * g1_ref[...])
    x1_ref[...] = x1
    h2 = _rms(x1) * g2_ref[...] * (1.0 + sc2_ref[...]) + sh2_ref[...]
    for j in range(d // LANES):
        h2_ref[pl.ds(j, tm, stride=SUBLANES), :] = h2[:, j * LANES:(j + 1) * LANES]

    h_hi, h_lo = _split(h2)
    hi_part = jnp.dot(h_hi, wrt_ref[...], preferred_element_type=F32)
    lg = hi_part[:, :LANES] + (hi_part[:, LANES:] + jnp.dot(h_lo, wrt_ref[:, :LANES], preferred_element_type=F32))
    lane = lax.broadcasted_iota(jnp.int32, lg.shape, 1)
    neg_inf = -jnp.inf
    gmask = lane < n_groups
    gmax = jnp.max(jnp.where(gmask, lg, neg_inf), axis=-1, keepdims=True)
    gsel = _lane_min_index(gmask & (lg == gmax), lane)
    gsum = jnp.sum(jnp.where(gmask, jnp.exp(lg - gmax), 0.0), axis=-1, keepdims=True)
    group_w = 1.0 / gsum
    lo = n_groups + gsel * per_group
    emask = (lane >= lo) & (lane < lo + per_group)
    m1 = jnp.max(jnp.where(emask, lg, neg_inf), axis=-1, keepdims=True)
    i1 = _lane_min_index(emask & (lg == m1), lane)
    rest = emask & (lane != i1)
    m2 = jnp.max(jnp.where(rest, lg, neg_inf), axis=-1, keepdims=True)
    i2 = _lane_min_index(rest & (lg == m2), lane)
    psum = jnp.sum(jnp.where(emask, jnp.exp(lg - m1), 0.0), axis=-1, keepdims=True)
    p1 = 1.0 / psum
    p2 = jnp.exp(m2 - m1) / psum
    w1 = group_w * (p1 / (p1 + p2))
    w2 = group_w * (p2 / (p1 + p2))
    e1 = (i1 - n_groups).astype(F32)
    e2 = (i2 - n_groups).astype(F32)
    route_ref[...] = jnp.where(lane == 0, e1, jnp.where(lane == 1, e2,
                               jnp.where(lane == 2, w1, jnp.where(lane == 3, w2, 0.0))))


def _merge(x2, ysb, yrw, gate_logits, gate_bias, wsb, wrw, wout, g1, gt1, g2, sc2, sh2, wrt,
           seq, n_groups, per_group):
    n, d = x2.shape
    tm = min(TM_MERGE, seq)
    tpb = seq // tm
    w = ysb.shape[1]
    row = lambda i: (i, 0)
    kernel = functools.partial(_merge_kernel, n_groups=n_groups, per_group=per_group)
    return pl.pallas_call(
        kernel,
        grid=(n // tm,),
        in_specs=[pl.BlockSpec((tm, d), row), pl.BlockSpec((tm, w), row), pl.BlockSpec((tm, w), row),
                  pl.BlockSpec((tm, 2 * d), row), _const_spec((1, 2 * d)),
                  _const_spec(wsb.shape), _const_spec(wrw.shape), _const_spec(wout.shape),
                  _const_spec((1, d)), _batch_spec(d, tpb), _const_spec((1, d)),
                  _batch_spec(d, tpb), _batch_spec(d, tpb), _const_spec(wrt.shape)],
        out_specs=[pl.BlockSpec((tm, d), row),
                   pl.BlockSpec((tm * SUBLANES, LANES), row),
                   pl.BlockSpec((tm, LANES), row)],
        out_shape=[jax.ShapeDtypeStruct((n, d), F32),
                   jax.ShapeDtypeStruct((n * SUBLANES, LANES), F32),
                   jax.ShapeDtypeStruct((n, LANES), F32)],
        compiler_params=_params("parallel"),
        name="merge",
    )(x2, ysb, yrw, gate_logits, gate_bias.reshape(1, -1), wsb, wrw, wout, g1, gt1, g2, sc2, sh2, wrt)


def _tile_rows(ref, index):
    return ref.at[pl.ds(pl.multiple_of(index * SUBLANES, SUBLANES), SUBLANES), :]


def _row_gather(src_hbm, idx_ref, base, dst, sem, n_rows):
    def body(r8, _):
        for u in range(GATHER_UNROLL):
            r = r8 * GATHER_UNROLL + u
            pltpu.make_async_copy(_tile_rows(src_hbm, idx_ref[base + r]), _tile_rows(dst, r), sem).start()
        return 0
    lax.fori_loop(0, n_rows // GATHER_UNROLL, body, 0)


def _row_gather_inline(src_hbm, idx_ref, base, dst, sem, n_rows, priorities=(0, 1)):
    for r in range(n_rows):
        pltpu.make_async_copy(_tile_rows(src_hbm, idx_ref[base + r]),
                              dst.at[pl.ds(r * SUBLANES, SUBLANES), :], sem).start(
                                  priority=priorities[r % len(priorities)])


def _row_gather_wait(src_hbm, dst, sem):
    pltpu.make_async_copy(src_hbm.at[pl.ds(0, dst.shape[0]), :], dst, sem).wait()


def _expert_kernel(tile_expert_ref, tile_valid_ref, row_tok_ref,
                   h2_hbm, w1_ref, w3_ref, wd_ref,
                   out_ref, gbuf_0, gbuf_1, gbuf_2, sems, w1_bf, w3_bf, wd_bf):
    tm = out_ref.shape[0] // SUBLANES
    d = w1_ref.shape[0]
    j = pl.program_id(0)
    phase = j % GATHER_BUFS
    bufs = (gbuf_0, gbuf_1, gbuf_2)
    valid = tile_valid_ref[j] == 1
    prev_valid = tile_valid_ref[jnp.maximum(j - 1, 0)] == 1

    @pl.when(jnp.logical_or(j == 0, tile_expert_ref[j] != tile_expert_ref[jnp.maximum(j - 1, 0)]))
    def _():
        w1_bf[...] = w1_ref[...].astype(BF16)
        w3_bf[...] = w3_ref[...].astype(BF16)
        wd_bf[...] = wd_ref[...].astype(BF16)

    @pl.when(j == 0)
    def _():
        for t in range(GATHER_AHEAD):
            _row_gather(h2_hbm, row_tok_ref, t * tm, bufs[t], sems.at[t], tm)

    def tile(p):
        cur = bufs[p]
        ahead = (p + GATHER_AHEAD) % GATHER_BUFS
        _row_gather_wait(h2_hbm, cur, sems.at[p])
        _row_gather_inline(h2_hbm, row_tok_ref, (j + GATHER_AHEAD) * tm, bufs[ahead], sems.at[ahead], tm,
                           priorities=(1,))
        x = jnp.concatenate([cur[pl.ds(c, tm, stride=SUBLANES), :].astype(BF16)
                             for c in range(d // LANES)], axis=1)
        up1 = jnp.dot(x, w1_bf[...], preferred_element_type=F32)
        up3 = jnp.dot(x, w3_bf[...], preferred_element_type=F32)
        hid = up1 * _sigmoid(up1) * up3
        y = jnp.dot(hid.astype(BF16), wd_bf[...], preferred_element_type=F32)
        for c in range(d // LANES):
            out_ref[pl.ds(c, tm, stride=SUBLANES), :] = y[:, c * LANES:(c + 1) * LANES]

    for p in range(GATHER_BUFS):
        pl.when(jnp.logical_and(valid, phase == p))(functools.partial(tile, p))

    @pl.when(jnp.logical_not(valid))
    def _():
        drain = jnp.logical_and(j > 0, prev_valid)
        for p in range(GATHER_BUFS):
            @pl.when(jnp.logical_and(drain, phase == p))
            def _():
                for t in range(GATHER_AHEAD):
                    q = (p + t) % GATHER_BUFS
                    _row_gather_wait(h2_hbm, bufs[q], sems.at[q])

        out_ref[...] = jnp.zeros_like(out_ref)


def _experts(h2_tiles, tile_expert, tile_valid, row_tok, w1, w3, wd):
    n_tiles = tile_expert.shape[0]
    tm = TM_EXP
    _, d, de = w1.shape
    rows = n_tiles * tm
    grid_spec = pltpu.PrefetchScalarGridSpec(
        num_scalar_prefetch=3,
        grid=(n_tiles,),
        in_specs=[pl.BlockSpec(memory_space=pl.ANY),
                  pl.BlockSpec((None, d, de), lambda j, te, tv, rt: (te[j], 0, 0)),
                  pl.BlockSpec((None, d, de), lambda j, te, tv, rt: (te[j], 0, 0)),
                  pl.BlockSpec((None, de, d), lambda j, te, tv, rt: (te[j], 0, 0))],
        out_specs=pl.BlockSpec((tm * SUBLANES, LANES), lambda j, te, tv, rt: (j, 0)),
        scratch_shapes=[pltpu.VMEM((tm * SUBLANES, LANES), F32) for _ in range(GATHER_BUFS)]
                       + [pltpu.SemaphoreType.DMA((GATHER_BUFS,)),
                        pltpu.VMEM((d, de), BF16), pltpu.VMEM((d, de), BF16), pltpu.VMEM((de, d), BF16)],
    )
    return pl.pallas_call(
        _expert_kernel,
        grid_spec=grid_spec,
        out_shape=jax.ShapeDtypeStruct((rows * SUBLANES, LANES), F32),
        compiler_params=_params("arbitrary"),
        name="experts",
    )(tile_expert, tile_valid, row_tok, h2_tiles, w1, w3, wd)


def _combine_kernel(pos_ref,
                    ye_hbm, x1_ref, route_ref, g3_ref, gt2_ref,
                    out_ref, gbuf_0, gbuf_1, gbuf_2, sems, *, n_tokens):
    tm, d = x1_ref.shape
    i = pl.program_id(0)
    nt = pl.num_programs(0)
    phase = i % GATHER_BUFS
    bufs = (gbuf_0, gbuf_1, gbuf_2)

    def fetch(tile_index, buf, sem, issue):
        for k in range(TOP_K_IN_GROUP):
            issue(ye_hbm, pos_ref, k * n_tokens + tile_index * tm, buf.at[k], sem, tm)

    @pl.when(i == 0)
    def _():
        for t in range(GATHER_AHEAD):
            fetch(jnp.minimum(t, nt - 1), bufs[t], sems.at[t], _row_gather)

    def tile(p):
        cur = bufs[p]
        for k in range(TOP_K_IN_GROUP):
            _row_gather_wait(ye_hbm, cur.at[k], sems.at[p])
        ahead = (p + GATHER_AHEAD) % GATHER_BUFS
        fetch(jnp.minimum(i + GATHER_AHEAD, nt - 1), bufs[ahead], sems.at[ahead], _row_gather_inline)
        wts = [route_ref[:, TOP_K_IN_GROUP + k:TOP_K_IN_GROUP + k + 1] for k in range(TOP_K_IN_GROUP)]
        y = jnp.concatenate(
            [sum(wts[k] * cur[k, pl.ds(c, tm, stride=SUBLANES), :] for k in range(TOP_K_IN_GROUP))
             for c in range(d // LANES)], axis=1)
        out_ref[...] = x1_ref[...] + gt2_ref[...] * (_rms(y) * g3_ref[...])

        @pl.when(i == nt - 1)
        def _():
            for t in range(1, GATHER_BUFS):
                q = (p + t) % GATHER_BUFS
                for k in range(TOP_K_IN_GROUP):
                    _row_gather_wait(ye_hbm, bufs[q].at[k], sems.at[q])

    for p in range(GATHER_BUFS):
        pl.when(phase == p)(functools.partial(tile, p))


def _combine(ye_tiles, pos, x1, route, g3, gt2, seq):
    n, d = x1.shape
    tm = min(TM_COMB, seq)
    tpb = seq // tm
    kernel = functools.partial(_combine_kernel, n_tokens=n)
    grid_spec = pltpu.PrefetchScalarGridSpec(
        num_scalar_prefetch=1,
        grid=(n // tm,),
        in_specs=[pl.BlockSpec(memory_space=pl.ANY),
                  pl.BlockSpec((tm, d), lambda i, p: (i, 0)),
                  pl.BlockSpec((tm, LANES), lambda i, p: (i, 0)),
                  pl.BlockSpec((1, d), lambda i, p: (0, 0)),
                  pl.BlockSpec((None, 1, d), lambda i, p: (i // tpb, 0, 0))],
        out_specs=pl.BlockSpec((tm, d), lambda i, p: (i, 0)),
        scratch_shapes=[pltpu.VMEM((TOP_K_IN_GROUP, tm * SUBLANES, LANES), F32) for _ in range(GATHER_BUFS)]
                       + [pltpu.SemaphoreType.DMA((GATHER_BUFS,))],
    )
    return pl.pallas_call(
        kernel,
        grid_spec=grid_spec,
        out_shape=jax.ShapeDtypeStruct((n, d), F32),
        compiler_params=_params("arbitrary"),
        name="combine",
    )(pos, ye_tiles, x1, route, g3, gt2)


def _plan_rows(route, n_experts, tm):
    n = route.shape[0]
    pairs = TOP_K_IN_GROUP * n
    expert = jnp.concatenate([route[:, k] for k in range(TOP_K_IN_GROUP)]).astype(jnp.int32)
    token = jnp.tile(jnp.arange(n, dtype=jnp.int32), TOP_K_IN_GROUP)
    onehot = (expert[:, None] == jnp.arange(n_experts, dtype=jnp.int32)[None, :]).astype(jnp.int32)
    rank = jnp.sum(jnp.cumsum(onehot, axis=0) * onehot, axis=1) - 1
    counts = jnp.sum(onehot, axis=0)
    padded = -(-counts // tm) * tm
    ends = jnp.cumsum(padded)
    offsets = ends - padded
    pos = jnp.sum(onehot * offsets[None, :], axis=1) + rank
    n_tiles = pairs // tm + n_experts + GATHER_AHEAD
    rows = n_tiles * tm
    row_tok = jnp.zeros((rows,), jnp.int32).at[pos].set(token, unique_indices=True, mode="promise_in_bounds")
    tile_start = jnp.arange(n_tiles, dtype=jnp.int32) * tm
    tile_expert = jnp.minimum(jnp.sum((ends[None, :] <= tile_start[:, None]).astype(jnp.int32), axis=1),
                              n_experts - 1)
    tile_valid = (tile_start < ends[-1]).astype(jnp.int32)
    return tile_expert, tile_valid, row_tok, pos.astype(jnp.int32)


def _layer(x, c, w_mod, b_mod, norm_gains, w_in, gate_bias, shift_mu, w0, w_decay_up, a0, w_iclr_up,
           w_gate_up, k_k, k_a, r_k, lnx_w, lnx_b, w_o_sb, w_o_rw, w_out, w_group, w_router,
           w_up1, w_up3, w_down):
    batch, seq, d = x.shape
    n = batch * seq
    sb_width = w_o_sb.shape[0]
    rw_width = w_o_rw.shape[0]
    decay_lora, iclr_lora, gate_lora = w_decay_up.shape[0], w_iclr_up.shape[0], w_gate_up.shape[0]
    lora = decay_lora + iclr_lora + gate_lora
    lora_pad = -(-lora // LANES) * LANES
    n_groups = w_group.shape[1]
    n_experts = w_router.shape[1]
    per_group = n_experts // n_groups

    mod = _modulation(c, w_mod, b_mod)
    sh1, sc1, gt1, sh2, sc2, gt2 = [m.reshape(batch, 1, d) for m in jnp.split(mod, 6, axis=-1)]
    gains = norm_gains.reshape(4, 1, d)

    o1 = 3 * sb_width
    o2 = o1 + 3 * rw_width
    o3 = o2 + lora
    wq = w_in[:, :o1].astype(BF16)
    wr = w_in[:, o1:o2].astype(BF16)
    wl = jnp.zeros((d, lora_pad), BF16).at[:, :lora].set(w_in[:, o2:o3].astype(BF16))
    wg = w_in[:, o3:].astype(BF16)
    x2 = x.reshape(n, d)
    qkv, rkv, lora_cols, gate_logits = _inproj(x2, gains[0], sc1, sh1, wq, wr, wl, wg, seq)

    y_sb = _sb_attention(qkv, batch, seq, sb_width)

    mu_rkv = shift_mu[:3 * rw_width]
    mu_lora = jnp.zeros((lora_pad,), F32).at[:lora].set(shift_mu[3 * rw_width:])
    l1 = decay_lora
    l2 = l1 + iclr_lora
    wd_pad = jnp.zeros((lora_pad, rw_width), F32).at[:l1].set(w_decay_up)
    wa_pad = jnp.zeros((lora_pad, rw_width), F32).at[l1:l2].set(w_iclr_up)
    wg_pad = jnp.zeros((lora_pad, rw_width), F32).at[l2:lora].set(w_gate_up)
    head = jnp.arange(rw_width) // HEAD_DIM
    hsum = (head[:, None] == head[None, :]).astype(BF16)
    q_c, yv_c, bonus, g, m_c, n_c = _rw_chunk(rkv, lora_cols, mu_rkv, mu_lora, w0, wd_pad, a0, wa_pad,
                                              wg_pad, k_k, k_a, r_k.reshape(-1), hsum, seq)
    y_rw = _rw_state(q_c, yv_c, bonus, g, m_c, n_c, lnx_w, lnx_b, hsum, batch, seq)

    wrt = jnp.zeros((d, LANES), F32).at[:, :n_groups].set(w_group)
    wrt = wrt.at[:, n_groups:n_groups + n_experts].set(w_router)
    wrt = jnp.concatenate(_split(wrt), axis=1)
    x1, h2_tiles, route = _merge(x2, y_sb, y_rw, gate_logits, gate_bias, w_o_sb.astype(BF16),
                                 w_o_rw.astype(BF16), w_out.astype(BF16), gains[1], gt1, gains[2],
                                 sc2, sh2, wrt, seq, n_groups, per_group)

    tile_expert, tile_valid, row_tok, pos = _plan_rows(route, n_experts, TM_EXP)
    ye = _experts(h2_tiles, tile_expert, tile_valid, row_tok,
                  w_up1, w_up3, w_down)
    out = _combine(ye, pos, x1, route, gains[3], gt2, seq)
    return out.reshape(batch, seq, d)


def kernel(x, c, w_mod, b_mod, norm_gains, w_in, gate_bias, shift_mu, w0, w_decay_up, a0, w_iclr_up,
           w_gate_up, k_k, k_a, r_k, lnx_w, lnx_b, w_o_sb, w_o_rw, w_out, w_group, w_router,
           w_up1, w_up3, w_down):
    for l in range(w_mod.shape[0]):
        x = _layer(x, c, w_mod[l], b_mod[l], norm_gains[l], w_in[l], gate_bias[l], shift_mu[l], w0[l],
                   w_decay_up[l], a0[l], w_iclr_up[l], w_gate_up[l], k_k[l], k_a[l], r_k[l], lnx_w[l],
                   lnx_b[l], w_o_sb[l], w_o_rw[l], w_out[l], w_group[l], w_router[l], w_up1[l],
                   w_up3[l], w_down[l])
    return x
```

```python
import functools
import math

import jax
import jax.numpy as jnp
from jax import lax
from jax.experimental import pallas as pl
from jax.experimental.pallas import tpu as pltpu

F32 = jnp.float32
BF16 = jnp.bfloat16
HIGHEST = lax.Precision.HIGHEST

HEAD_DIM = 64
TOP_K_IN_GROUP = 2
NORM_EPS = 1e-6
GN_EPS = 64e-5
L2_EPS = 1e-12

LANES = 128
SUBLANES = 8
VMEM_LIMIT_BYTES = 56 * 1024 * 1024

TM_PROJ = 512
T_ATT = 256
ATT_DEAD_LOG2 = -150.0
C_SCAN = 64
TM_CHUNK = 256
TM_MERGE = 512
TM_EXP = 256
TM_COMB = 256
GATHER_UNROLL = 8
GATHER_AHEAD = 2
GATHER_BUFS = GATHER_AHEAD + 1

_NN = ((1,), (0,))
_NT = ((1,), (1,))
_TN = ((0,), (0,))


def _params(*sem):
    return pltpu.CompilerParams(dimension_semantics=sem, vmem_limit_bytes=VMEM_LIMIT_BYTES)


def _hdot(a, b):
    return jnp.dot(a, b, precision=HIGHEST, preferred_element_type=F32)


def _split(x):
    hi = x.astype(BF16)
    return hi, (x - hi.astype(F32)).astype(BF16)


def _mm1(a, b, dims=_NN):
    return lax.dot_general(a.astype(BF16), b.astype(BF16), (dims, ((), ())), preferred_element_type=F32)


def _mm01(a, b, exact):
    if exact == "a":
        hi, lo = _split(b)
        return jnp.dot(a, hi, preferred_element_type=F32) + jnp.dot(a, lo, preferred_element_type=F32)
    hi, lo = _split(a)
    return jnp.dot(hi, b, preferred_element_type=F32) + jnp.dot(lo, b, preferred_element_type=F32)


def _rms(x):
    return x * lax.rsqrt(jnp.mean(x * x, axis=-1, keepdims=True) + NORM_EPS)


def _sigmoid(x):
    return 1.0 / (1.0 + jnp.exp(-x))


def _mod_kernel(c_ref, w_ref, b_ref, o_ref):
    c = c_ref[...]
    o_ref[...] = _hdot(c * _sigmoid(c), w_ref[...]) + b_ref[...]


def _modulation(c, w_mod, b_mod):
    b, d = c.shape
    n_out = w_mod.shape[1]
    rows = -(-b // SUBLANES) * SUBLANES
    c_pad = jnp.zeros((rows, d), F32).at[:b].set(c)
    out = pl.pallas_call(
        _mod_kernel,
        grid=(n_out // d,),
        in_specs=[pl.BlockSpec((rows, d), lambda j: (0, 0)),
                  pl.BlockSpec((d, d), lambda j: (0, j)),
                  pl.BlockSpec((1, d), lambda j: (0, j))],
        out_specs=pl.BlockSpec((rows, d), lambda j: (0, j)),
        out_shape=jax.ShapeDtypeStruct((rows, n_out), F32),
        compiler_params=_params("parallel"),
        name="mod",
    )(c_pad, w_mod, b_mod.reshape(1, n_out))
    return out[:b]


def _inproj_kernel(x_ref, g_ref, sc_ref, sh_ref, wq_ref, wr_ref, wl_ref, wg_ref,
                   qkv_ref, rkv_ref, lora_ref, gate_ref):
    h = _rms(x_ref[...]) * g_ref[...] * (1.0 + sc_ref[...]) + sh_ref[...]
    hb = h.astype(BF16)
    qkv_ref[...] = jnp.dot(hb, wq_ref[...], preferred_element_type=F32).astype(qkv_ref.dtype)
    rkv_ref[...] = jnp.dot(hb, wr_ref[...], preferred_element_type=F32)
    lora_ref[...] = jnp.dot(hb, wl_ref[...], preferred_element_type=F32)
    gate_ref[...] = jnp.dot(hb, wg_ref[...], preferred_element_type=F32)


def _const_spec(shape):
    return pl.BlockSpec(shape, lambda *_: (0,) * len(shape))


def _batch_spec(d, tiles_per_batch):
    return pl.BlockSpec((None, 1, d), lambda i: (i // tiles_per_batch, 0, 0))


def _inproj(x2, gain, sc, sh, wq, wr, wl, wg, seq):
    n, d = x2.shape
    tm = min(TM_PROJ, seq)
    tpb = seq // tm
    outs = [(wq.shape[1], BF16), (wr.shape[1], F32), (wl.shape[1], F32), (wg.shape[1], F32)]
    return pl.pallas_call(
        _inproj_kernel,
        grid=(n // tm,),
        in_specs=[pl.BlockSpec((tm, d), lambda i: (i, 0)),
                  _const_spec((1, d)), _batch_spec(d, tpb), _batch_spec(d, tpb),
                  _const_spec(wq.shape), _const_spec(wr.shape), _const_spec(wl.shape),
                  _const_spec(wg.shape)],
        out_specs=[pl.BlockSpec((tm, w), lambda i: (i, 0)) for w, _ in outs],
        out_shape=[jax.ShapeDtypeStruct((n, w), dt) for w, dt in outs],
        compiler_params=_params("parallel"),
        name="inproj",
    )(x2, gain, sc, sh, wq, wr, wl, wg)


def _sb_attn_kernel(q_ref, k_ref, v_ref, o_ref, acc_ref, *, scale):
    t = q_ref.shape[0]
    qi = pl.program_id(2)
    row = lax.broadcasted_iota(jnp.int32, (t, t), 0)
    col = lax.broadcasted_iota(jnp.int32, (t, t), 1)
    tri = jnp.where(row > col, 1.0, 0.0).astype(BF16)
    strict = col < row
    lane = lax.broadcasted_iota(jnp.int32, (t, LANES), 1)
    first = lane < HEAD_DIM
    q2 = q_ref[...].astype(F32) * (scale * math.log2(math.e))
    qs = jnp.concatenate([jnp.where(first, q2, 0.0), jnp.where(first, 0.0, q2)], axis=0).astype(BF16)
    strict2 = jnp.concatenate([strict, strict], axis=0)

    def blocks(kbs, run, masked):
        k2 = [k_ref[pl.ds(pl.multiple_of(kb * t, t), t), :] for kb in kbs]
        v2 = [v_ref[pl.ds(pl.multiple_of(kb * t, t), t), :] for kb in kbs]
        z = [lax.dot_general(qs, k2[j], (_NT, ((), ())), preferred_element_type=F32) for j in range(len(kbs))]
        for j in range(len(kbs)):
            zz = z[j]
            neg_part = jnp.minimum(zz, 0.0)
            neg_other = neg_part - zz
            soft = jnp.log2(1.0 + jnp.exp2(neg_part + neg_other))
            log_beta = neg_part - soft
            log_keep = neg_other - soft
            if masked[j]:
                log_keep = jnp.where(strict2, log_keep, 0.0)
            later = jnp.dot(log_keep.astype(BF16), tri, preferred_element_type=F32)
            att = jnp.exp2(log_beta + later + run)
            if masked[j]:
                att = jnp.where(strict2, att, 0.0)
            acc_ref[...] += jnp.dot(att.astype(BF16), v2[j], preferred_element_type=F32)
            run = run + jnp.sum(log_keep, axis=-1, keepdims=True)
        return run

    acc_ref[...] = jnp.zeros_like(acc_ref)
    zero = jnp.zeros((2 * t, 1), F32)
    run = lax.cond(qi > 0,
                   lambda: blocks([qi, qi - 1], zero, (True, False)),
                   lambda: blocks([qi], zero, (True,)))

    def live(state):
        kb, r = state
        return jnp.logical_and(kb >= 0, jnp.max(r) > ATT_DEAD_LOG2)

    def body(state):
        kb, r = state
        return kb - 1, blocks([kb], r, (False,))

    lax.while_loop(live, body, (qi - 2, run))
    o_ref[...] = jnp.where(first, acc_ref[:t, :], acc_ref[t:, :]).astype(o_ref.dtype)


def _sb_attention(qkv, batch, seq, width):
    n = qkv.shape[0]
    t = min(T_ATT, seq)
    n_pairs = width // LANES
    nq = seq // t
    kernel = functools.partial(_sb_attn_kernel, scale=1.0 / math.sqrt(HEAD_DIM))
    return pl.pallas_call(
        kernel,
        grid=(batch, n_pairs, nq),
        in_specs=[pl.BlockSpec((t, LANES), lambda b, p, i: (b * nq + i, p)),
                  pl.BlockSpec((seq, LANES), lambda b, p, i: (b, n_pairs + p)),
                  pl.BlockSpec((seq, LANES), lambda b, p, i: (b, 2 * n_pairs + p))],
        out_specs=pl.BlockSpec((t, LANES), lambda b, p, i: (b * nq + i, p)),
        out_shape=jax.ShapeDtypeStruct((n, width), BF16),
        scratch_shapes=[pltpu.VMEM((2 * t, LANES), F32)],
        compiler_params=_params("parallel", "parallel", "parallel"),
        name="sb_attn",
    )(qkv, qkv, qkv)


def _shift_rows(p, halo_row):
    rolled = pltpu.roll(p, 1, 0)
    row = lax.broadcasted_iota(jnp.int32, p.shape, 0)
    return jnp.where(row == 0, halo_row, rolled)


def _rw_chunk_kernel(rkv_ref, rkv_halo_ref, lora_ref, lora_halo_ref, mu_rkv_ref, mu_lora_ref,
                     w0_ref, wd_ref, a0_ref, wa_ref, wg_ref, kk_ref, ka_ref, rk_ref, hsum_ref,
                     q_out, yv_out, bonus_out, g_out, m_out, n_out, *, tiles_per_batch, chunk):
    i = pl.program_id(0)
    keep = jnp.where(i % tiles_per_batch == 0, 0.0, 1.0)
    w = w0_ref.shape[1]
    tm = rkv_ref.shape[0]
    c = chunk
    hd = HEAD_DIM
    n_heads = w // hd

    p = rkv_ref[...]
    prev = _shift_rows(p, rkv_halo_ref[SUBLANES - 1:SUBLANES, :] * keep)
    xs = p + mu_rkv_ref[...] * (prev - p)
    pl_ = lora_ref[...]
    prevl = _shift_rows(pl_, lora_halo_ref[SUBLANES - 1:SUBLANES, :] * keep)
    xl = pl_ + mu_lora_ref[...] * (prevl - pl_)

    r = xs[:, :w]
    k = xs[:, w:2 * w]
    v = xs[:, 2 * w:]
    dec_in = w0_ref[...] + _mm1(jnp.tanh(xl), wd_ref[...])
    neg = -dec_in
    softplus = jnp.maximum(neg, 0.0) + jnp.log(1.0 + jnp.exp(-jnp.abs(neg)))
    lw = -jnp.exp(-softplus - 0.5)
    a_gate = _sigmoid(a0_ref[...] + _mm1(xl, wa_ref[...]))
    g_out[...] = _mm1(_sigmoid(xl), wg_ref[...])
    kk = k * kk_ref[...]
    k = k * (1.0 + (a_gate - 1.0) * ka_ref[...])
    hsum = hsum_ref[...]
    kkn = kk / jnp.maximum(jnp.sqrt(_mm01(kk * kk, hsum, "b")), L2_EPS)
    a = -kkn
    b = kkn * a_gate
    bonus_out[...] = _mm01(r * k * rk_ref[...], hsum, "b") * v

    row = lax.broadcasted_iota(jnp.int32, (c, c), 0)
    col = lax.broadcasted_iota(jnp.int32, (c, c), 1)
    incl = col <= row
    strict = col < row
    tril = jnp.where(incl, 1.0, 0.0).astype(BF16)
    eye = jnp.where(row == col, 1.0, 0.0)

    a_t, r_t, b_t, k_t, b_h, k_h, v_p, g_c = [], [], [], [], [], [], [], []
    for g in range(tm // c):
        rs = slice(g * c, (g + 1) * c)
        lw_c = lw[rs]
        cum = _mm01(tril, lw_c, "a")
        total = cum[c - 1:c, :]
        inv = jnp.exp(-cum)
        tail = jnp.exp(total - cum)
        full = {"a_t": a[rs] * jnp.exp(cum - lw_c), "r_t": r[rs] * jnp.exp(cum), "b_t": b[rs] * inv,
                "k_t": k[rs] * inv, "b_h": b[rs] * tail, "k_h": k[rs] * tail, "v": v[rs],
                "g_c": jnp.exp(total)}
        for h in range(n_heads):
            sl = slice(h * hd, (h + 1) * hd)
            a_t.append(full["a_t"][:, sl]); r_t.append(full["r_t"][:, sl])
            b_t.append(full["b_t"][:, sl]); k_t.append(full["k_t"][:, sl])
            b_h.append(full["b_h"][:, sl]); k_h.append(full["k_h"][:, sl])
            v_p.append(full["v"][:, sl]); g_c.append(full["g_c"][:, sl])
    n_prob = len(a_t)
    prob = range(n_prob)

    lhs = [jnp.concatenate([a_t[j], r_t[j]], axis=0) for j in prob]
    gram_b = [_mm1(lhs[j], b_t[j], _NT) for j in prob]
    gram_k = [_mm1(lhs[j], k_t[j], _NT) for j in prob]
    a_ab = [jnp.where(strict, gram_b[j][:c], 0.0) for j in prob]
    a_rb = [jnp.where(incl, gram_b[j][c:], 0.0) for j in prob]
    a_ak = [jnp.where(strict, gram_k[j][:c], 0.0) for j in prob]
    a_rk = [jnp.where(incl, gram_k[j][c:], 0.0) for j in prob]
    ak_v = [_mm1(a_ak[j], v_p[j]) for j in prob]
    rk_v = [_mm1(a_rk[j], v_p[j]) for j in prob]
    wu = [jnp.concatenate([a_t[j], ak_v[j]], axis=1) for j in prob]
    power = a_ab
    n_apply = max(1, math.ceil(math.log2(c)))
    for step in range(n_apply):
        wu = [wu[j] + _mm1(power[j], wu[j]) for j in prob]
        if step + 1 < n_apply:
            power = [_mm1(power[j], power[j]) for j in prob]
    qy = [_mm1(a_rb[j], wu[j]) + jnp.concatenate([r_t[j], rk_v[j]], axis=1) for j in prob]
    wu_b = [_mm1(wu[j], b_h[j], _TN) for j in prob]
    v_k = [_mm1(v_p[j], k_h[j], _TN) for j in prob]
    for j in prob:
        g, h = divmod(j, n_heads)
        rs = slice(g * c, (g + 1) * c)
        sl = slice(h * hd, (h + 1) * hd)
        q_out[rs, sl] = qy[j][:, :hd]
        yv_out[rs, sl] = qy[j][:, hd:]
        m_out[g, h] = wu_b[j][:hd] + eye * g_c[j]
        n_out[g, h] = wu_b[j][hd:] + v_k[j]


def _rw_chunk(rkv, lora, mu_rkv, mu_lora, w0, wd_pad, a0, wa_pad, wg_pad, k_k, k_a, r_k, hsum, seq):
    n, w3 = rkv.shape
    w = w3 // 3
    wl = lora.shape[1]
    c = min(C_SCAN, seq)
    tm = min(TM_CHUNK, seq)
    tpb = seq // tm
    cpt = tm // c
    hb = tm // SUBLANES
    n_heads = w // HEAD_DIM
    kernel = functools.partial(_rw_chunk_kernel, tiles_per_batch=tpb, chunk=c)
    row = lambda i: (i, 0)
    halo = lambda i: (jnp.maximum(i * hb - 1, 0), 0)
    vec = lambda a: a.reshape(1, -1)
    mat = pl.BlockSpec((cpt, n_heads, HEAD_DIM, HEAD_DIM), lambda i: (i, 0, 0, 0))
    mat_shape = jax.ShapeDtypeStruct((n // c, n_heads, HEAD_DIM, HEAD_DIM), F32)
    return pl.pallas_call(
        kernel,
        grid=(n // tm,),
        in_specs=[pl.BlockSpec((tm, w3), row), pl.BlockSpec((SUBLANES, w3), halo),
                  pl.BlockSpec((tm, wl), row), pl.BlockSpec((SUBLANES, wl), halo),
                  _const_spec((1, w3)), _const_spec((1, wl)),
                  _const_spec((1, w)), _const_spec((wl, w)), _const_spec((1, w)),
                  _const_spec((wl, w)), _const_spec((wl, w)),
                  _const_spec((1, w)), _const_spec((1, w)), _const_spec((1, w)),
                  _const_spec((w, w))],
        out_specs=[pl.BlockSpec((tm, w), row)] * 4 + [mat, mat],
        out_shape=[jax.ShapeDtypeStruct((n, w), F32)] * 4 + [mat_shape, mat_shape],
        compiler_params=_params("parallel"),
        name="rw_chunk",
    )(rkv, rkv, lora, lora, vec(mu_rkv), vec(mu_lora), vec(w0), wd_pad, vec(a0), wa_pad, wg_pad,
      vec(k_k), vec(k_a), vec(r_k), hsum)


def _rw_state_kernel(q_ref, yv_ref, bonus_ref, g_ref, m_ref, n_ref, lnw_ref, lnb_ref, hsum_ref,
                     o_ref, state_ref):
    n_batch, c, w = q_ref.shape
    hd = HEAD_DIM
    n_heads = w // hd

    @pl.when(pl.program_id(0) == 0)
    def _():
        state_ref[...] = jnp.zeros_like(state_ref)

    pairs = [(bi, h) for bi in range(n_batch) for h in range(n_heads)]
    state = [state_ref[bi, h] for bi, h in pairs]
    y = [_mm1(q_ref[bi, :, h * hd:(h + 1) * hd], state[j], _NT) + yv_ref[bi, :, h * hd:(h + 1) * hd]
         for j, (bi, h) in enumerate(pairs)]
    new_state = [_mm1(state[j], m_ref[bi, h]) + n_ref[bi, h] for j, (bi, h) in enumerate(pairs)]
    for j, (bi, h) in enumerate(pairs):
        state_ref[bi, h] = new_state[j]

    hsum = hsum_ref[...]
    y_all = jnp.concatenate([jnp.concatenate(y[bi * n_heads:(bi + 1) * n_heads], axis=1)
                             for bi in range(n_batch)], axis=0)
    mean = _mm01(y_all, hsum, "b") * (1.0 / hd)
    dev = y_all - mean
    var = _mm01(dev * dev, hsum, "b") * (1.0 / hd)
    yn = dev * lax.rsqrt(var + GN_EPS) * lnw_ref[...] + lnb_ref[...]
    for bi in range(n_batch):
        rows = slice(bi * c, (bi + 1) * c)
        o_ref[bi] = ((yn[rows] + bonus_ref[bi]) * g_ref[bi]).astype(o_ref.dtype)


def _rw_state(q, yv, bonus, g, m, n_mat, lnx_w, lnx_b, hsum, batch, seq):
    w = q.shape[1]
    c = min(C_SCAN, seq)
    nc = seq // c
    n_heads = w // HEAD_DIM
    rows = pl.BlockSpec((batch, c, w), lambda ci: (0, ci, 0))
    mats = pl.BlockSpec((batch, None, n_heads, HEAD_DIM, HEAD_DIM), lambda ci: (0, ci, 0, 0, 0))
    to3 = lambda a: a.reshape(batch, seq, w)
    to5 = lambda a: a.reshape(batch, nc, n_heads, HEAD_DIM, HEAD_DIM)
    out = pl.pallas_call(
        _rw_state_kernel,
        grid=(nc,),
        in_specs=[rows] * 4 + [mats, mats, _const_spec((1, w)), _const_spec((1, w)), _const_spec((w, w))],
        out_specs=rows,
        out_shape=jax.ShapeDtypeStruct((batch, seq, w), BF16),
        scratch_shapes=[pltpu.VMEM((batch, n_heads, HEAD_DIM, HEAD_DIM), F32)],
        compiler_params=_params("arbitrary"),
        name="rw_state",
    )(to3(q), to3(yv), to3(bonus), to3(g), to5(m), to5(n_mat), lnx_w.reshape(1, w), lnx_b.reshape(1, w), hsum)
    return out.reshape(batch * seq, w)


def _lane_min_index(mask, lane):
    return jnp.min(jnp.where(mask, lane, 4 * LANES), axis=-1, keepdims=True)


def _merge_kernel(x_ref, ysb_ref, yrw_ref, gl_ref, gb_ref, wsb_ref, wrw_ref, wout_ref,
                  g1_ref, gt1_ref, g2_ref, sc2_ref, sh2_ref, wrt_ref,
                  x1_ref, h2_ref, route_ref, *, n_groups, per_group):
    d = x_ref.shape[1]
    tm = x_ref.shape[0]
    gates = _sigmoid(gl_ref[...] + gb_ref[...])
    merged = (gates[:, :d] * jnp.dot(ysb_ref[...], wsb_ref[...], preferred_element_type=F32)
              + gates[:, d:] * jnp.dot(yrw_ref[...], wrw_ref[...], preferred_element_type=F32))
    y = jnp.dot(merged.astype(BF16), wout_ref[...], preferred_element_type=F32)
    x1 = x_ref[...] + gt1_ref[...] * (_rms(y) * g1_ref[...])
    x1_ref[...] = x1
    h2 = _rms(x1) * g2_ref[...] * (1.0 + sc2_ref[...]) + sh2_ref[...]
    for j in range(d // LANES):
        h2_ref[pl.ds(j, tm, stride=SUBLANES), :] = h2[:, j * LANES:(j + 1) * LANES]

    h_hi, h_lo = _split(h2)
    hi_part = jnp.dot(h_hi, wrt_ref[...], preferred_element_type=F32)
    lg = hi_part[:, :LANES] + (hi_part[:, LANES:] + jnp.dot(h_lo, wrt_ref[:, :LANES], preferred_element_type=F32))
    lane = lax.broadcasted_iota(jnp.int32, lg.shape, 1)
    neg_inf = -jnp.inf
    gmask = lane < n_groups
    gmax = jnp.max(jnp.where(gmask, lg, neg_inf), axis=-1, keepdims=True)
    gsel = _lane_min_index(gmask & (lg == gmax), lane)
    gsum = jnp.sum(jnp.where(gmask, jnp.exp(lg - gmax), 0.0), axis=-1, keepdims=True)
    group_w = 1.0 / gsum
    lo = n_groups + gsel * per_group
    emask = (lane >= lo) & (lane < lo + per_group)
    m1 = jnp.max(jnp.where(emask, lg, neg_inf), axis=-1, keepdims=True)
    i1 = _lane_min_index(emask & (lg == m1), lane)
    rest = emask & (lane != i1)
    m2 = jnp.max(jnp.where(rest, lg, neg_inf), axis=-1, keepdims=True)
    i2 = _lane_min_index(rest & (lg == m2), lane)
    psum = jnp.sum(jnp.where(emask, jnp.exp(lg - m1), 0.0), axis=-1, keepdims=True)
    p1 = 1.0 / psum
    p2 = jnp.exp(m2 - m1) / psum
    w1 = group_w * (p1 / (p1 + p2))
    w2 = group_w * (p2 / (p1 + p2))
    e1 = (i1 - n_groups).astype(F32)
    e2 = (i2 - n_groups).astype(F32)
    route_ref[...] = jnp.where(lane == 0, e1, jnp.where(lane == 1, e2,
                               jnp.where(lane == 2, w1, jnp.where(lane == 3, w2, 0.0))))


def _merge(x2, ysb, yrw, gate_logits, gate_bias, wsb, wrw, wout, g1, gt1, g2, sc2, sh2, wrt,
           seq, n_groups, per_group):
    n, d = x2.shape
    tm = min(TM_MERGE, seq)
    tpb = seq // tm
    w = ysb.shape[1]
    row = lambda i: (i, 0)
    kernel = functools.partial(_merge_kernel, n_groups=n_groups, per_group=per_group)
    return pl.pallas_call(
        kernel,
        grid=(n // tm,),
        in_specs=[pl.BlockSpec((tm, d), row), pl.BlockSpec((tm, w), row), pl.BlockSpec((tm, w), row),
                  pl.BlockSpec((tm, 2 * d), row), _const_spec((1, 2 * d)),
                  _const_spec(wsb.shape), _const_spec(wrw.shape), _const_spec(wout.shape),
                  _const_spec((1, d)), _batch_spec(d, tpb), _const_spec((1, d)),
                  _batch_spec(d, tpb), _batch_spec(d, tpb), _const_spec(wrt.shape)],
        out_specs=[pl.BlockSpec((tm, d), row),
                   pl.BlockSpec((tm * SUBLANES, LANES), row),
                   pl.BlockSpec((tm, LANES), row)],
        out_shape=[jax.ShapeDtypeStruct((n, d), F32),
                   jax.ShapeDtypeStruct((n * SUBLANES, LANES), F32),
                   jax.ShapeDtypeStruct((n, LANES), F32)],
        compiler_params=_params("parallel"),
        name="merge",
    )(x2, ysb, yrw, gate_logits, gate_bias.reshape(1, -1), wsb, wrw, wout, g1, gt1, g2, sc2, sh2, wrt)


def _tile_rows(ref, index):
    return ref.at[pl.ds(pl.multiple_of(index * SUBLANES, SUBLANES), SUBLANES), :]


def _row_gather(src_hbm, idx_ref, base, dst, sem, n_rows):
    def body(r8, _):
        for u in range(GATHER_UNROLL):
            r = r8 * GATHER_UNROLL + u
            pltpu.make_async_copy(_tile_rows(src_hbm, idx_ref[base + r]), _tile_rows(dst, r), sem).start()
        return 0
    lax.fori_loop(0, n_rows // GATHER_UNROLL, body, 0)


def _row_gather_inline(src_hbm, idx_ref, base, dst, sem, n_rows, priorities=(0, 1)):
    for r in range(n_rows):
        pltpu.make_async_copy(_tile_rows(src_hbm, idx_ref[base + r]),
                              dst.at[pl.ds(r * SUBLANES, SUBLANES), :], sem).start(
                                  priority=priorities[r % len(priorities)])


def _row_gather_wait(src_hbm, dst, sem):
    pltpu.make_async_copy(src_hbm.at[pl.ds(0, dst.shape[0]), :], dst, sem).wait()


def _expert_kernel(tile_expert_ref, tile_valid_ref, row_tok_ref,
                   h2_hbm, w1_ref, w3_ref, wd_ref,
                   out_ref, gbuf_0, gbuf_1, gbuf_2, sems, w1_bf, w3_bf, wd_bf):
    tm = out_ref.shape[0] // SUBLANES
    d = w1_ref.shape[0]
    j = pl.program_id(0)
    phase = j % GATHER_BUFS
    bufs = (gbuf_0, gbuf_1, gbuf_2)
    valid = tile_valid_ref[j] == 1
    prev_valid = tile_valid_ref[jnp.maximum(j - 1, 0)] == 1

    @pl.when(jnp.logical_or(j == 0, tile_expert_ref[j] != tile_expert_ref[jnp.maximum(j - 1, 0)]))
    def _():
        w1_bf[...] = w1_ref[...].astype(BF16)
        w3_bf[...] = w3_ref[...].astype(BF16)
        wd_bf[...] = wd_ref[...].astype(BF16)

    @pl.when(j == 0)
    def _():
        for t in range(GATHER_AHEAD):
            _row_gather(h2_hbm, row_tok_ref, t * tm, bufs[t], sems.at[t], tm)

    def tile(p):
        cur = bufs[p]
        ahead = (p + GATHER_AHEAD) % GATHER_BUFS
        _row_gather_wait(h2_hbm, cur, sems.at[p])
        _row_gather_inline(h2_hbm, row_tok_ref, (j + GATHER_AHEAD) * tm, bufs[ahead], sems.at[ahead], tm,
                           priorities=(1,))
        x = jnp.concatenate([cur[pl.ds(c, tm, stride=SUBLANES), :].astype(BF16)
                             for c in range(d // LANES)], axis=1)
        up1 = jnp.dot(x, w1_bf[...], preferred_element_type=F32)
        up3 = jnp.dot(x, w3_bf[...], preferred_element_type=F32)
        hid = up1 * _sigmoid(up1) * up3
        y = jnp.dot(hid.astype(BF16), wd_bf[...], preferred_element_type=F32)
        for c in range(d // LANES):
            out_ref[pl.ds(c, tm, stride=SUBLANES), :] = y[:, c * LANES:(c + 1) * LANES]

    for p in range(GATHER_BUFS):
        pl.when(jnp.logical_and(valid, phase == p))(functools.partial(tile, p))

    @pl.when(jnp.logical_not(valid))
    def _():
        drain = jnp.logical_and(j > 0, prev_valid)
        for p in range(GATHER_BUFS):
            @pl.when(jnp.logical_and(drain, phase == p))
            def _():
                for t in range(GATHER_AHEAD):
                    q = (p + t) % GATHER_BUFS
                    _row_gather_wait(h2_hbm, bufs[q], sems.at[q])

        out_ref[...] = jnp.zeros_like(out_ref)


def _experts(h2_tiles, tile_expert, tile_valid, row_tok, w1, w3, wd):
    n_tiles = tile_expert.shape[0]
    tm = TM_EXP
    _, d, de = w1.shape
    rows = n_tiles * tm
    grid_spec = pltpu.PrefetchScalarGridSpec(
        num_scalar_prefetch=3,
        grid=(n_tiles,),
        in_specs=[pl.BlockSpec(memory_space=pl.ANY),
                  pl.BlockSpec((None, d, de), lambda j, te, tv, rt: (te[j], 0, 0)),
                  pl.BlockSpec((None, d, de), lambda j, te, tv, rt: (te[j], 0, 0)),
                  pl.BlockSpec((None, de, d), lambda j, te, tv, rt: (te[j], 0, 0))],
        out_specs=pl.BlockSpec((tm * SUBLANES, LANES), lambda j, te, tv, rt: (j, 0)),
        scratch_shapes=[pltpu.VMEM((tm * SUBLANES, LANES), F32) for _ in range(GATHER_BUFS)]
                       + [pltpu.SemaphoreType.DMA((GATHER_BUFS,)),
                        pltpu.VMEM((d, de), BF16), pltpu.VMEM((d, de), BF16), pltpu.VMEM((de, d), BF16)],
    )
    return pl.pallas_call(
        _expert_kernel,
        grid_spec=grid_spec,
        out_shape=jax.ShapeDtypeStruct((rows * SUBLANES, LANES), F32),
        compiler_params=_params("arbitrary"),
        name="experts",
    )(tile_expert, tile_valid, row_tok, h2_tiles, w1, w3, wd)


def _combine_kernel(pos_ref,
                    ye_hbm, x1_ref, route_ref, g3_ref, gt2_ref,
                    out_ref, gbuf_0, gbuf_1, gbuf_2, sems, *, n_tokens):
    tm, d = x1_ref.shape
    i = pl.program_id(0)
    nt = pl.num_programs(0)
    phase = i % GATHER_BUFS
    bufs = (gbuf_0, gbuf_1, gbuf_2)

    def fetch(tile_index, buf, sem, issue):
        for k in range(TOP_K_IN_GROUP):
            issue(ye_hbm, pos_ref, k * n_tokens + tile_index * tm, buf.at[k], sem, tm)

    @pl.when(i == 0)
    def _():
        for t in range(GATHER_AHEAD):
            fetch(jnp.minimum(t, nt - 1), bufs[t], sems.at[t], _row_gather)

    def tile(p):
        cur = bufs[p]
        for k in range(TOP_K_IN_GROUP):
            _row_gather_wait(ye_hbm, cur.at[k], sems.at[p])
        ahead = (p + GATHER_AHEAD) % GATHER_BUFS
        fetch(jnp.minimum(i + GATHER_AHEAD, nt - 1), bufs[ahead], sems.at[ahead], _row_gather_inline)
        wts = [route_ref[:, TOP_K_IN_GROUP + k:TOP_K_IN_GROUP + k + 1] for k in range(TOP_K_IN_GROUP)]
        y = jnp.concatenate(
            [sum(wts[k] * cur[k, pl.ds(c, tm, stride=SUBLANES), :] for k in range(TOP_K_IN_GROUP))
             for c in range(d // LANES)], axis=1)
        out_ref[...] = x1_ref[...] + gt2_ref[...] * (_rms(y) * g3_ref[...])

        @pl.when(i == nt - 1)
        def _():
            for t in range(1, GATHER_BUFS):
                q = (p + t) % GATHER_BUFS
                for k in range(TOP_K_IN_GROUP):
                    _row_gather_wait(ye_hbm, bufs[q].at[k], sems.at[q])

    for p in range(GATHER_BUFS):
        pl.when(phase == p)(functools.partial(tile, p))


def _combine(ye_tiles, pos, x1, route, g3, gt2, seq):
    n, d = x1.shape
    tm = min(TM_COMB, seq)
    tpb = seq // tm
    kernel = functools.partial(_combine_kernel, n_tokens=n)
    grid_spec = pltpu.PrefetchScalarGridSpec(
        num_scalar_prefetch=1,
        grid=(n // tm,),
        in_specs=[pl.BlockSpec(memory_space=pl.ANY),
                  pl.BlockSpec((tm, d), lambda i, p: (i, 0)),
                  pl.BlockSpec((tm, LANES), lambda i, p: (i, 0)),
                  pl.BlockSpec((1, d), lambda i, p: (0, 0)),
                  pl.BlockSpec((None, 1, d), lambda i, p: (i // tpb, 0, 0))],
        out_specs=pl.BlockSpec((tm, d), lambda i, p: (i, 0)),
        scratch_shapes=[pltpu.VMEM((TOP_K_IN_GROUP, tm * SUBLANES, LANES), F32) for _ in range(GATHER_BUFS)]
                       + [pltpu.SemaphoreType.DMA((GATHER_BUFS,))],
    )
    return pl.pallas_call(
        kernel,
        grid_spec=grid_spec,
        out_shape=jax.ShapeDtypeStruct((n, d), F32),
        compiler_params=_params("arbitrary"),
        name="combine",
    )(pos, ye_tiles, x1, route, g3, gt2)


def _plan_rows(route, n_experts, tm):
    n = route.shape[0]
    pairs = TOP_K_IN_GROUP * n
    expert = jnp.concatenate([route[:, k] for k in range(TOP_K_IN_GROUP)]).astype(jnp.int32)
    token = jnp.tile(jnp.arange(n, dtype=jnp.int32), TOP_K_IN_GROUP)
    onehot = (expert[:, None] == jnp.arange(n_experts, dtype=jnp.int32)[None, :]).astype(jnp.int32)
    rank = jnp.sum(jnp.cumsum(onehot, axis=0) * onehot, axis=1) - 1
    counts = jnp.sum(onehot, axis=0)
    padded = -(-counts // tm) * tm
    ends = jnp.cumsum(padded)
    offsets = ends - padded
    pos = jnp.sum(onehot * offsets[None, :], axis=1) + rank
    n_tiles = pairs // tm + n_experts + GATHER_AHEAD
    rows = n_tiles * tm
    row_tok = jnp.zeros((rows,), jnp.int32).at[pos].set(token, unique_indices=True, mode="promise_in_bounds")
    tile_start = jnp.arange(n_tiles, dtype=jnp.int32) * tm
    tile_expert = jnp.minimum(jnp.sum((ends[None, :] <= tile_start[:, None]).astype(jnp.int32), axis=1),
                              n_experts - 1)
    tile_valid = (tile_start < ends[-1]).astype(jnp.int32)
    return tile_expert, tile_valid, row_tok, pos.astype(jnp.int32)


def _layer(x, c, w_mod, b_mod, norm_gains, w_in, gate_bias, shift_mu, w0, w_decay_up, a0, w_iclr_up,
           w_gate_up, k_k, k_a, r_k, lnx_w, lnx_b, w_o_sb, w_o_rw, w_out, w_group, w_router,
           w_up1, w_up3, w_down):
    batch, seq, d = x.shape
    n = batch * seq
    sb_width = w_o_sb.shape[0]
    rw_width = w_o_rw.shape[0]
    decay_lora, iclr_lora, gate_lora = w_decay_up.shape[0], w_iclr_up.shape[0], w_gate_up.shape[0]
    lora = decay_lora + iclr_lora + gate_lora
    lora_pad = -(-lora // LANES) * LANES
    n_groups = w_group.shape[1]
    n_experts = w_router.shape[1]
    per_group = n_experts // n_groups

    mod = _modulation(c, w_mod, b_mod)
    sh1, sc1, gt1, sh2, sc2, gt2 = [m.reshape(batch, 1, d) for m in jnp.split(mod, 6, axis=-1)]
    gains = norm_gains.reshape(4, 1, d)

    o1 = 3 * sb_width
    o2 = o1 + 3 * rw_width
    o3 = o2 + lora
    wq = w_in[:, :o1].astype(BF16)
    wr = w_in[:, o1:o2].astype(BF16)
    wl = jnp.zeros((d, lora_pad), BF16).at[:, :lora].set(w_in[:, o2:o3].astype(BF16))
    wg = w_in[:, o3:].astype(BF16)
    x2 = x.reshape(n, d)
    qkv, rkv, lora_cols, gate_logits = _inproj(x2, gains[0], sc1, sh1, wq, wr, wl, wg, seq)

    y_sb = _sb_attention(qkv, batch, seq, sb_width)

    mu_rkv = shift_mu[:3 * rw_width]
    mu_lora = jnp.zeros((lora_pad,), F32).at[:lora].set(shift_mu[3 * rw_width:])
    l1 = decay_lora
    l2 = l1 + iclr_lora
    wd_pad = jnp.zeros((lora_pad, rw_width), F32).at[:l1].set(w_decay_up)
    wa_pad = jnp.zeros((lora_pad, rw_width), F32).at[l1:l2].set(w_iclr_up)
    wg_pad = jnp.zeros((lora_pad, rw_width), F32).at[l2:lora].set(w_gate_up)
    head = jnp.arange(rw_width) // HEAD_DIM
    hsum = (head[:, None] == head[None, :]).astype(BF16)
    q_c, yv_c, bonus, g, m_c, n_c = _rw_chunk(rkv, lora_cols, mu_rkv, mu_lora, w0, wd_pad, a0, wa_pad,
                                              wg_pad, k_k, k_a, r_k.reshape(-1), hsum, seq)
    y_rw = _rw_state(q_c, yv_c, bonus, g, m_c, n_c, lnx_w, lnx_b, hsum, batch, seq)

    wrt = jnp.zeros((d, LANES), F32).at[:, :n_groups].set(w_group)
    wrt = wrt.at[:, n_groups:n_groups + n_experts].set(w_router)
    wrt = jnp.concatenate(_split(wrt), axis=1)
    x1, h2_tiles, route = _merge(x2, y_sb, y_rw, gate_logits, gate_bias, w_o_sb.astype(BF16),
                                 w_o_rw.astype(BF16), w_out.astype(BF16), gains[1], gt1, gains[2],
                                 sc2, sh2, wrt, seq, n_groups, per_group)

    tile_expert, tile_valid, row_tok, pos = _plan_rows(route, n_experts, TM_EXP)
    ye = _experts(h2_tiles, tile_expert, tile_valid, row_tok,
                  w_up1, w_up3, w_down)
    out = _combine(ye, pos, x1, route, gains[3], gt2, seq)
    return out.reshape(batch, seq, d)


def kernel(x, c, w_mod, b_mod, norm_gains, w_in, gate_bias, shift_mu, w0, w_decay_up, a0, w_iclr_up,
           w_gate_up, k_k, k_a, r_k, lnx_w, lnx_b, w_o_sb, w_o_rw, w_out, w_group, w_router,
           w_up1, w_up3, w_down):
    for l in range(w_mod.shape[0]):
        x = _layer(x, c, w_mod[l], b_mod[l], norm_gains[l], w_in[l], gate_bias[l], shift_mu[l], w0[l],
                   w_decay_up[l], a0[l], w_iclr_up[l], w_gate_up[l], k_k[l], k_a[l], r_k[l], lnx_w[l],
                   lnx_b[l], w_o_sb[l], w_o_rw[l], w_out[l], w_group[l], w_router[l], w_up1[l],
                   w_up3[l], w_down[l])
    return x
```

```python
import functools
import math

import jax
import jax.numpy as jnp
from jax import lax
from jax.experimental import pallas as pl
from jax.experimental.pallas import tpu as pltpu

F32 = jnp.float32
BF16 = jnp.bfloat16
HIGHEST = lax.Precision.HIGHEST

HEAD_DIM = 64
TOP_K_IN_GROUP = 2
NORM_EPS = 1e-6
GN_EPS = 64e-5
L2_EPS = 1e-12

LANES = 128
SUBLANES = 8
VMEM_LIMIT_BYTES = 56 * 1024 * 1024

TM_PROJ = 512
T_ATT = 256
ATT_DEAD_LOG2 = -150.0
C_SCAN = 64
TM_CHUNK = 256
TM_MERGE = 512
TM_EXP = 256
TM_COMB = 256
GATHER_UNROLL = 8
GATHER_AHEAD = 2
GATHER_BUFS = GATHER_AHEAD + 1
SCATTER_UNROLL = 8

_NN = ((1,), (0,))
_NT = ((1,), (1,))
_TN = ((0,), (0,))


def _params(*sem):
    return pltpu.CompilerParams(dimension_semantics=sem, vmem_limit_bytes=VMEM_LIMIT_BYTES)


def _hdot(a, b):
    return jnp.dot(a, b, precision=HIGHEST, preferred_element_type=F32)


def _split(x):
    hi = x.astype(BF16)
    return hi, (x - hi.astype(F32)).astype(BF16)


def _mm1(a, b, dims=_NN):
    return lax.dot_general(a.astype(BF16), b.astype(BF16), (dims, ((), ())), preferred_element_type=F32)


def _mm01(a, b, exact):
    if exact == "a":
        hi, lo = _split(b)
        return jnp.dot(a, hi, preferred_element_type=F32) + jnp.dot(a, lo, preferred_element_type=F32)
    hi, lo = _split(a)
    return jnp.dot(hi, b, preferred_element_type=F32) + jnp.dot(lo, b, preferred_element_type=F32)


def _rms(x):
    return x * lax.rsqrt(jnp.mean(x * x, axis=-1, keepdims=True) + NORM_EPS)


def _sigmoid(x):
    return 1.0 / (1.0 + jnp.exp(-x))


def _mod_kernel(c_ref, w_ref, b_ref, o_ref):
    c = c_ref[...]
    o_ref[...] = _hdot(c * _sigmoid(c), w_ref[...]) + b_ref[...]


def _modulation(c, w_mod, b_mod):
    b, d = c.shape
    n_out = w_mod.shape[1]
    rows = -(-b // SUBLANES) * SUBLANES
    c_pad = jnp.zeros((rows, d), F32).at[:b].set(c)
    out = pl.pallas_call(
        _mod_kernel,
        grid=(n_out // d,),
        in_specs=[pl.BlockSpec((rows, d), lambda j: (0, 0)),
                  pl.BlockSpec((d, d), lambda j: (0, j)),
                  pl.BlockSpec((1, d), lambda j: (0, j))],
        out_specs=pl.BlockSpec((rows, d), lambda j: (0, j)),
        out_shape=jax.ShapeDtypeStruct((rows, n_out), F32),
        compiler_params=_params("parallel"),
        name="mod",
    )(c_pad, w_mod, b_mod.reshape(1, n_out))
    return out[:b]


def _inproj_kernel(x_ref, g_ref, sc_ref, sh_ref, wq_ref, wr_ref, wl_ref, wg_ref,
                   qkv_ref, rkv_ref, lora_ref, gate_ref):
    h = _rms(x_ref[...]) * g_ref[...] * (1.0 + sc_ref[...]) + sh_ref[...]
    hb = h.astype(BF16)
    qkv_ref[...] = jnp.dot(hb, wq_ref[...], preferred_element_type=F32).astype(qkv_ref.dtype)
    rkv_ref[...] = jnp.dot(hb, wr_ref[...], preferred_element_type=F32)
    lora_ref[...] = jnp.dot(hb, wl_ref[...], preferred_element_type=F32)
    gate_ref[...] = jnp.dot(hb, wg_ref[...], preferred_element_type=F32)


def _const_spec(shape):
    return pl.BlockSpec(shape, lambda *_: (0,) * len(shape))


def _batch_spec(d, tiles_per_batch):
    return pl.BlockSpec((None, 1, d), lambda i: (i // tiles_per_batch, 0, 0))


def _inproj(x2, gain, sc, sh, wq, wr, wl, wg, seq):
    n, d = x2.shape
    tm = min(TM_PROJ, seq)
    tpb = seq // tm
    outs = [(wq.shape[1], BF16), (wr.shape[1], F32), (wl.shape[1], F32), (wg.shape[1], F32)]
    return pl.pallas_call(
        _inproj_kernel,
        grid=(n // tm,),
        in_specs=[pl.BlockSpec((tm, d), lambda i: (i, 0)),
                  _const_spec((1, d)), _batch_spec(d, tpb), _batch_spec(d, tpb),
                  _const_spec(wq.shape), _const_spec(wr.shape), _const_spec(wl.shape),
                  _const_spec(wg.shape)],
        out_specs=[pl.BlockSpec((tm, w), lambda i: (i, 0)) for w, _ in outs],
        out_shape=[jax.ShapeDtypeStruct((n, w), dt) for w, dt in outs],
        compiler_params=_params("parallel"),
        name="inproj",
    )(x2, gain, sc, sh, wq, wr, wl, wg)


def _sb_attn_kernel(q_ref, k_ref, v_ref, o_ref, acc_ref, *, scale):
    t = q_ref.shape[0]
    qi = pl.program_id(2)
    row = lax.broadcasted_iota(jnp.int32, (t, t), 0)
    col = lax.broadcasted_iota(jnp.int32, (t, t), 1)
    tri = jnp.where(row > col, 1.0, 0.0).astype(BF16)
    strict = col < row
    lane = lax.broadcasted_iota(jnp.int32, (t, LANES), 1)
    first = lane < HEAD_DIM
    q2 = q_ref[...].astype(F32) * (scale * math.log2(math.e))
    qs = jnp.concatenate([jnp.where(first, q2, 0.0), jnp.where(first, 0.0, q2)], axis=0).astype(BF16)
    strict2 = jnp.concatenate([strict, strict], axis=0)

    def blocks(kbs, run, masked):
        k2 = [k_ref[pl.ds(pl.multiple_of(kb * t, t), t), :] for kb in kbs]
        v2 = [v_ref[pl.ds(pl.multiple_of(kb * t, t), t), :] for kb in kbs]
        z = [lax.dot_general(qs, k2[j], (_NT, ((), ())), preferred_element_type=F32) for j in range(len(kbs))]
        for j in range(len(kbs)):
            zz = z[j]
            neg_part = jnp.minimum(zz, 0.0)
            neg_other = neg_part - zz
            soft = jnp.log2(1.0 + jnp.exp2(neg_part + neg_other))
            log_beta = neg_part - soft
            log_keep = neg_other - soft
            if masked[j]:
                log_keep = jnp.where(strict2, log_keep, 0.0)
            later = jnp.dot(log_keep.astype(BF16), tri, preferred_element_type=F32)
            att = jnp.exp2(log_beta + later + run)
            if masked[j]:
                att = jnp.where(strict2, att, 0.0)
            acc_ref[...] += jnp.dot(att.astype(BF16), v2[j], preferred_element_type=F32)
            run = run + jnp.sum(log_keep, axis=-1, keepdims=True)
        return run

    acc_ref[...] = jnp.zeros_like(acc_ref)
    zero = jnp.zeros((2 * t, 1), F32)
    run = lax.cond(qi > 0,
                   lambda: blocks([qi, qi - 1], zero, (True, False)),
                   lambda: blocks([qi], zero, (True,)))

    def live(state):
        kb, r = state
        return jnp.logical_and(kb >= 0, jnp.max(r) > ATT_DEAD_LOG2)

    def body(state):
        kb, r = state
        return kb - 1, blocks([kb], r, (False,))

    lax.while_loop(live, body, (qi - 2, run))
    o_ref[...] = jnp.where(first, acc_ref[:t, :], acc_ref[t:, :]).astype(o_ref.dtype)


def _sb_attention(qkv, batch, seq, width):
    n = qkv.shape[0]
    t = min(T_ATT, seq)
    n_pairs = width // LANES
    nq = seq // t
    kernel = functools.partial(_sb_attn_kernel, scale=1.0 / math.sqrt(HEAD_DIM))
    return pl.pallas_call(
        kernel,
        grid=(batch, n_pairs, nq),
        in_specs=[pl.BlockSpec((t, LANES), lambda b, p, i: (b * nq + i, p)),
                  pl.BlockSpec((seq, LANES), lambda b, p, i: (b, n_pairs + p)),
                  pl.BlockSpec((seq, LANES), lambda b, p, i: (b, 2 * n_pairs + p))],
        out_specs=pl.BlockSpec((t, LANES), lambda b, p, i: (b * nq + i, p)),
        out_shape=jax.ShapeDtypeStruct((n, width), BF16),
        scratch_shapes=[pltpu.VMEM((2 * t, LANES), F32)],
        compiler_params=_params("parallel", "parallel", "parallel"),
        name="sb_attn",
    )(qkv, qkv, qkv)


def _shift_rows(p, halo_row):
    rolled = pltpu.roll(p, 1, 0)
    row = lax.broadcasted_iota(jnp.int32, p.shape, 0)
    return jnp.where(row == 0, halo_row, rolled)


def _rw_chunk_kernel(rkv_ref, rkv_halo_ref, lora_ref, lora_halo_ref, mu_rkv_ref, mu_lora_ref,
                     w0_ref, wd_ref, a0_ref, wa_ref, wg_ref, kk_ref, ka_ref, rk_ref, hsum_ref,
                     q_out, yv_out, bonus_out, g_out, m_out, n_out, *, tiles_per_batch, chunk):
    i = pl.program_id(0)
    keep = jnp.where(i % tiles_per_batch == 0, 0.0, 1.0)
    w = w0_ref.shape[1]
    tm = rkv_ref.shape[0]
    c = chunk
    hd = HEAD_DIM
    n_heads = w // hd

    p = rkv_ref[...]
    prev = _shift_rows(p, rkv_halo_ref[SUBLANES - 1:SUBLANES, :] * keep)
    xs = p + mu_rkv_ref[...] * (prev - p)
    pl_ = lora_ref[...]
    prevl = _shift_rows(pl_, lora_halo_ref[SUBLANES - 1:SUBLANES, :] * keep)
    xl = pl_ + mu_lora_ref[...] * (prevl - pl_)

    r = xs[:, :w]
    k = xs[:, w:2 * w]
    v = xs[:, 2 * w:]
    dec_in = w0_ref[...] + _mm1(jnp.tanh(xl), wd_ref[...])
    neg = -dec_in
    softplus = jnp.maximum(neg, 0.0) + jnp.log(1.0 + jnp.exp(-jnp.abs(neg)))
    lw = -jnp.exp(-softplus - 0.5)
    a_gate = _sigmoid(a0_ref[...] + _mm1(xl, wa_ref[...]))
    g_out[...] = _mm1(_sigmoid(xl), wg_ref[...])
    kk = k * kk_ref[...]
    k = k * (1.0 + (a_gate - 1.0) * ka_ref[...])
    hsum = hsum_ref[...]
    kkn = kk / jnp.maximum(jnp.sqrt(_mm01(kk * kk, hsum, "b")), L2_EPS)
    a = -kkn
    b = kkn * a_gate
    bonus_out[...] = _mm01(r * k * rk_ref[...], hsum, "b") * v

    row = lax.broadcasted_iota(jnp.int32, (c, c), 0)
    col = lax.broadcasted_iota(jnp.int32, (c, c), 1)
    incl = col <= row
    strict = col < row
    tril = jnp.where(incl, 1.0, 0.0).astype(BF16)
    eye = jnp.where(row == col, 1.0, 0.0)

    a_t, r_t, b_t, k_t, b_h, k_h, v_p, g_c = [], [], [], [], [], [], [], []
    for g in range(tm // c):
        rs = slice(g * c, (g + 1) * c)
        lw_c = lw[rs]
        cum = _mm01(tril, lw_c, "a")
        total = cum[c - 1:c, :]
        inv = jnp.exp(-cum)
        tail = jnp.exp(total - cum)
        full = {"a_t": a[rs] * jnp.exp(cum - lw_c), "r_t": r[rs] * jnp.exp(cum), "b_t": b[rs] * inv,
                "k_t": k[rs] * inv, "b_h": b[rs] * tail, "k_h": k[rs] * tail, "v": v[rs],
                "g_c": jnp.exp(total)}
        for h in range(n_heads):
            sl = slice(h * hd, (h + 1) * hd)
            a_t.append(full["a_t"][:, sl]); r_t.append(full["r_t"][:, sl])
            b_t.append(full["b_t"][:, sl]); k_t.append(full["k_t"][:, sl])
            b_h.append(full["b_h"][:, sl]); k_h.append(full["k_h"][:, sl])
            v_p.append(full["v"][:, sl]); g_c.append(full["g_c"][:, sl])
    n_prob = len(a_t)
    prob = range(n_prob)

    lhs = [jnp.concatenate([a_t[j], r_t[j]], axis=0) for j in prob]
    gram_b = [_mm1(lhs[j], b_t[j], _NT) for j in prob]
    gram_k = [_mm1(lhs[j], k_t[j], _NT) for j in prob]
    a_ab = [jnp.where(strict, gram_b[j][:c], 0.0) for j in prob]
    a_rb = [jnp.where(incl, gram_b[j][c:], 0.0) for j in prob]
    a_ak = [jnp.where(strict, gram_k[j][:c], 0.0) for j in prob]
    a_rk = [jnp.where(incl, gram_k[j][c:], 0.0) for j in prob]
    ak_v = [_mm1(a_ak[j], v_p[j]) for j in prob]
    rk_v = [_mm1(a_rk[j], v_p[j]) for j in prob]
    wu = [jnp.concatenate([a_t[j], ak_v[j]], axis=1) for j in prob]
    power = a_ab
    n_apply = max(1, math.ceil(math.log2(c)))
    for step in range(n_apply):
        wu = [wu[j] + _mm1(power[j], wu[j]) for j in prob]
        if step + 1 < n_apply:
            power = [_mm1(power[j], power[j]) for j in prob]
    qy = [_mm1(a_rb[j], wu[j]) + jnp.concatenate([r_t[j], rk_v[j]], axis=1) for j in prob]
    wu_b = [_mm1(wu[j], b_h[j], _TN) for j in prob]
    v_k = [_mm1(v_p[j], k_h[j], _TN) for j in prob]
    for j in prob:
        g, h = divmod(j, n_heads)
        rs = slice(g * c, (g + 1) * c)
        sl = slice(h * hd, (h + 1) * hd)
        q_out[rs, sl] = qy[j][:, :hd]
        yv_out[rs, sl] = qy[j][:, hd:]
        m_out[g, h] = wu_b[j][:hd] + eye * g_c[j]
        n_out[g, h] = wu_b[j][hd:] + v_k[j]


def _rw_chunk(rkv, lora, mu_rkv, mu_lora, w0, wd_pad, a0, wa_pad, wg_pad, k_k, k_a, r_k, hsum, seq):
    n, w3 = rkv.shape
    w = w3 // 3
    wl = lora.shape[1]
    c = min(C_SCAN, seq)
    tm = min(TM_CHUNK, seq)
    tpb = seq // tm
    cpt = tm // c
    hb = tm // SUBLANES
    n_heads = w // HEAD_DIM
    kernel = functools.partial(_rw_chunk_kernel, tiles_per_batch=tpb, chunk=c)
    row = lambda i: (i, 0)
    halo = lambda i: (jnp.maximum(i * hb - 1, 0), 0)
    vec = lambda a: a.reshape(1, -1)
    mat = pl.BlockSpec((cpt, n_heads, HEAD_DIM, HEAD_DIM), lambda i: (i, 0, 0, 0))
    mat_shape = jax.ShapeDtypeStruct((n // c, n_heads, HEAD_DIM, HEAD_DIM), F32)
    return pl.pallas_call(
        kernel,
        grid=(n // tm,),
        in_specs=[pl.BlockSpec((tm, w3), row), pl.BlockSpec((SUBLANES, w3), halo),
                  pl.BlockSpec((tm, wl), row), pl.BlockSpec((SUBLANES, wl), halo),
                  _const_spec((1, w3)), _const_spec((1, wl)),
                  _const_spec((1, w)), _const_spec((wl, w)), _const_spec((1, w)),
                  _const_spec((wl, w)), _const_spec((wl, w)),
                  _const_spec((1, w)), _const_spec((1, w)), _const_spec((1, w)),
                  _const_spec((w, w))],
        out_specs=[pl.BlockSpec((tm, w), row)] * 4 + [mat, mat],
        out_shape=[jax.ShapeDtypeStruct((n, w), F32)] * 4 + [mat_shape, mat_shape],
        compiler_params=_params("parallel"),
        name="rw_chunk",
    )(rkv, rkv, lora, lora, vec(mu_rkv), vec(mu_lora), vec(w0), wd_pad, vec(a0), wa_pad, wg_pad,
      vec(k_k), vec(k_a), vec(r_k), hsum)


def _rw_state_kernel(q_ref, yv_ref, bonus_ref, g_ref, m_ref, n_ref, lnw_ref, lnb_ref, hsum_ref,
                     o_ref, state_ref):
    n_batch, c, w = q_ref.shape
    hd = HEAD_DIM
    n_heads = w // hd

    @pl.when(pl.program_id(0) == 0)
    def _():
        state_ref[...] = jnp.zeros_like(state_ref)

    pairs = [(bi, h) for bi in range(n_batch) for h in range(n_heads)]
    state = [state_ref[bi, h] for bi, h in pairs]
    y = [_mm1(q_ref[bi, :, h * hd:(h + 1) * hd], state[j], _NT) + yv_ref[bi, :, h * hd:(h + 1) * hd]
         for j, (bi, h) in enumerate(pairs)]
    new_state = [_mm1(state[j], m_ref[bi, h]) + n_ref[bi, h] for j, (bi, h) in enumerate(pairs)]
    for j, (bi, h) in enumerate(pairs):
        state_ref[bi, h] = new_state[j]

    hsum = hsum_ref[...]
    y_all = jnp.concatenate([jnp.concatenate(y[bi * n_heads:(bi + 1) * n_heads], axis=1)
                             for bi in range(n_batch)], axis=0)
    mean = _mm01(y_all, hsum, "b") * (1.0 / hd)
    dev = y_all - mean
    var = _mm01(dev * dev, hsum, "b") * (1.0 / hd)
    yn = dev * lax.rsqrt(var + GN_EPS) * lnw_ref[...] + lnb_ref[...]
    for bi in range(n_batch):
        rows = slice(bi * c, (bi + 1) * c)
        o_ref[bi] = ((yn[rows] + bonus_ref[bi]) * g_ref[bi]).astype(o_ref.dtype)


def _rw_state(q, yv, bonus, g, m, n_mat, lnx_w, lnx_b, hsum, batch, seq):
    w = q.shape[1]
    c = min(C_SCAN, seq)
    nc = seq // c
    n_heads = w // HEAD_DIM
    rows = pl.BlockSpec((batch, c, w), lambda ci: (0, ci, 0))
    mats = pl.BlockSpec((batch, None, n_heads, HEAD_DIM, HEAD_DIM), lambda ci: (0, ci, 0, 0, 0))
    to3 = lambda a: a.reshape(batch, seq, w)
    to5 = lambda a: a.reshape(batch, nc, n_heads, HEAD_DIM, HEAD_DIM)
    out = pl.pallas_call(
        _rw_state_kernel,
        grid=(nc,),
        in_specs=[rows] * 4 + [mats, mats, _const_spec((1, w)), _const_spec((1, w)), _const_spec((w, w))],
        out_specs=rows,
        out_shape=jax.ShapeDtypeStruct((batch, seq, w), BF16),
        scratch_shapes=[pltpu.VMEM((batch, n_heads, HEAD_DIM, HEAD_DIM), F32)],
        compiler_params=_params("arbitrary"),
        name="rw_state",
    )(to3(q), to3(yv), to3(bonus), to3(g), to5(m), to5(n_mat), lnx_w.reshape(1, w), lnx_b.reshape(1, w), hsum)
    return out.reshape(batch * seq, w)


def _lane_min_index(mask, lane):
    return jnp.min(jnp.where(mask, lane, 4 * LANES), axis=-1, keepdims=True)


def _merge_kernel(x_ref, ysb_ref, yrw_ref, gl_ref, gb_ref, wsb_ref, wrw_ref, wout_ref,
                  g1_ref, gt1_ref, g2_ref, sc2_ref, sh2_ref, wrt_ref,
                  x1_ref, h2_ref, route_ref, *, n_groups, per_group):
    d = x_ref.shape[1]
    tm = x_ref.shape[0]
    gates = _sigmoid(gl_ref[...] + gb_ref[...])
    merged = (gates[:, :d] * jnp.dot(ysb_ref[...], wsb_ref[...], preferred_element_type=F32)
              + gates[:, d:] * jnp.dot(yrw_ref[...], wrw_ref[...], preferred_element_type=F32))
    y = jnp.dot(merged.astype(BF16), wout_ref[...], preferred_element_type=F32)
    x1 = x_ref[...] + gt1_ref[...] * (_rms(y) * g1_ref[...])
    x1_ref[...] = x1
    h2 = _rms(x1) * g2_ref[...] * (1.0 + sc2_ref[...]) + sh2_ref[...]
    for j in range(d // LANES):
        h2_ref[pl.ds(j, tm, stride=SUBLANES), :] = h2[:, j * LANES:(j + 1) * LANES]

    h_hi, h_lo = _split(h2)
    hi_part = jnp.dot(h_hi, wrt_ref[...], preferred_element_type=F32)
    lg = hi_part[:, :LANES] + (hi_part[:, LANES:] + jnp.dot(h_lo, wrt_ref[:, :LANES], preferred_element_type=F32))
    lane = lax.broadcasted_iota(jnp.int32, lg.shape, 1)
    neg_inf = -jnp.inf
    gmask = lane < n_groups
    gmax = jnp.max(jnp.where(gmask, lg, neg_inf), axis=-1, keepdims=True)
    gsel = _lane_min_index(gmask & (lg == gmax), lane)
    gsum = jnp.sum(jnp.where(gmask, jnp.exp(lg - gmax), 0.0), axis=-1, keepdims=True)
    group_w = 1.0 / gsum
    lo = n_groups + gsel * per_group
    emask = (lane >= lo) & (lane < lo + per_group)
    m1 = jnp.max(jnp.where(emask, lg, neg_inf), axis=-1, keepdims=True)
    i1 = _lane_min_index(emask & (lg == m1), lane)
    rest = emask & (lane != i1)
    m2 = jnp.max(jnp.where(rest, lg, neg_inf), axis=-1, keepdims=True)
    i2 = _lane_min_index(rest & (lg == m2), lane)
    psum = jnp.sum(jnp.where(emask, jnp.exp(lg - m1), 0.0), axis=-1, keepdims=True)
    p1 = 1.0 / psum
    p2 = jnp.exp(m2 - m1) / psum
    w1 = group_w * (p1 / (p1 + p2))
    w2 = group_w * (p2 / (p1 + p2))
    e1 = (i1 - n_groups).astype(F32)
    e2 = (i2 - n_groups).astype(F32)
    route_ref[...] = jnp.where(lane == 0, e1, jnp.where(lane == 1, e2,
                               jnp.where(lane == 2, w1, jnp.where(lane == 3, w2, 0.0))))


def _merge(x2, ysb, yrw, gate_logits, gate_bias, wsb, wrw, wout, g1, gt1, g2, sc2, sh2, wrt,
           seq, n_groups, per_group):
    n, d = x2.shape
    tm = min(TM_MERGE, seq)
    tpb = seq // tm
    w = ysb.shape[1]
    row = lambda i: (i, 0)
    kernel = functools.partial(_merge_kernel, n_groups=n_groups, per_group=per_group)
    return pl.pallas_call(
        kernel,
        grid=(n // tm,),
        in_specs=[pl.BlockSpec((tm, d), row), pl.BlockSpec((tm, w), row), pl.BlockSpec((tm, w), row),
                  pl.BlockSpec((tm, 2 * d), row), _const_spec((1, 2 * d)),
                  _const_spec(wsb.shape), _const_spec(wrw.shape), _const_spec(wout.shape),
                  _const_spec((1, d)), _batch_spec(d, tpb), _const_spec((1, d)),
                  _batch_spec(d, tpb), _batch_spec(d, tpb), _const_spec(wrt.shape)],
        out_specs=[pl.BlockSpec((tm, d), row),
                   pl.BlockSpec((tm * SUBLANES, LANES), row),
                   pl.BlockSpec((tm, LANES), row)],
        out_shape=[jax.ShapeDtypeStruct((n, d), F32),
                   jax.ShapeDtypeStruct((n * SUBLANES, LANES), F32),
                   jax.ShapeDtypeStruct((n, LANES), F32)],
        compiler_params=_params("parallel"),
        name="merge",
    )(x2, ysb, yrw, gate_logits, gate_bias.reshape(1, -1), wsb, wrw, wout, g1, gt1, g2, sc2, sh2, wrt)


def _tile_rows(ref, index):
    return ref.at[pl.ds(pl.multiple_of(index * SUBLANES, SUBLANES), SUBLANES), :]


def _row_gather(src_hbm, idx_ref, base, dst, sem, n_rows):
    def body(r8, _):
        for u in range(GATHER_UNROLL):
            r = r8 * GATHER_UNROLL + u
            pltpu.make_async_copy(_tile_rows(src_hbm, idx_ref[base + r]), _tile_rows(dst, r), sem).start()
        return 0
    lax.fori_loop(0, n_rows // GATHER_UNROLL, body, 0)


def _row_gather_inline(src_hbm, idx_ref, base, dst, sem, n_rows, priorities=(0, 1)):
    for r in range(n_rows):
        pltpu.make_async_copy(_tile_rows(src_hbm, idx_ref[base + r]),
                              dst.at[pl.ds(r * SUBLANES, SUBLANES), :], sem).start(
                                  priority=priorities[r % len(priorities)])


def _row_gather_wait(src_hbm, dst, sem):
    pltpu.make_async_copy(src_hbm.at[pl.ds(0, dst.shape[0]), :], dst, sem).wait()


def _expert_kernel(tile_expert_ref, tile_valid_ref, row_tok_ref,
                   h2_hbm, w1_ref, w3_ref, wd_ref,
                   out_ref, gbuf_0, gbuf_1, gbuf_2, sems, w1_bf, w3_bf, wd_bf):
    tm = out_ref.shape[0] // SUBLANES
    d = w1_ref.shape[0]
    j = pl.program_id(0)
    phase = j % GATHER_BUFS
    bufs = (gbuf_0, gbuf_1, gbuf_2)
    valid = tile_valid_ref[j] == 1
    prev_valid = tile_valid_ref[jnp.maximum(j - 1, 0)] == 1

    @pl.when(jnp.logical_or(j == 0, tile_expert_ref[j] != tile_expert_ref[jnp.maximum(j - 1, 0)]))
    def _():
        w1_bf[...] = w1_ref[...].astype(BF16)
        w3_bf[...] = w3_ref[...].astype(BF16)
        wd_bf[...] = wd_ref[...].astype(BF16)

    @pl.when(j == 0)
    def _():
        for t in range(GATHER_AHEAD):
            _row_gather(h2_hbm, row_tok_ref, t * tm, bufs[t], sems.at[t], tm)

    def tile(p):
        cur = bufs[p]
        ahead = (p + GATHER_AHEAD) % GATHER_BUFS
        _row_gather_wait(h2_hbm, cur, sems.at[p])
        _row_gather_inline(h2_hbm, row_tok_ref, (j + GATHER_AHEAD) * tm, bufs[ahead], sems.at[ahead], tm,
                           priorities=(1,))
        x = jnp.concatenate([cur[pl.ds(c, tm, stride=SUBLANES), :].astype(BF16)
                             for c in range(d // LANES)], axis=1)
        up1 = jnp.dot(x, w1_bf[...], preferred_element_type=F32)
        up3 = jnp.dot(x, w3_bf[...], preferred_element_type=F32)
        hid = up1 * _sigmoid(up1) * up3
        y = jnp.dot(hid.astype(BF16), wd_bf[...], preferred_element_type=F32)
        for c in range(d // LANES):
            out_ref[pl.ds(c, tm, stride=SUBLANES), :] = y[:, c * LANES:(c + 1) * LANES]

    for p in range(GATHER_BUFS):
        pl.when(jnp.logical_and(valid, phase == p))(functools.partial(tile, p))

    @pl.when(jnp.logical_not(valid))
    def _():
        drain = jnp.logical_and(j > 0, prev_valid)
        for p in range(GATHER_BUFS):
            @pl.when(jnp.logical_and(drain, phase == p))
            def _():
                for t in range(GATHER_AHEAD):
                    q = (p + t) % GATHER_BUFS
                    _row_gather_wait(h2_hbm, bufs[q], sems.at[q])

        out_ref[...] = jnp.zeros_like(out_ref)


def _experts(h2_tiles, tile_expert, tile_valid, row_tok, w1, w3, wd):
    n_tiles = tile_expert.shape[0]
    tm = TM_EXP
    _, d, de = w1.shape
    rows = n_tiles * tm
    grid_spec = pltpu.PrefetchScalarGridSpec(
        num_scalar_prefetch=3,
        grid=(n_tiles,),
        in_specs=[pl.BlockSpec(memory_space=pl.ANY),
                  pl.BlockSpec((None, d, de), lambda j, te, tv, rt: (te[j], 0, 0)),
                  pl.BlockSpec((None, d, de), lambda j, te, tv, rt: (te[j], 0, 0)),
                  pl.BlockSpec((None, de, d), lambda j, te, tv, rt: (te[j], 0, 0))],
        out_specs=pl.BlockSpec((tm * SUBLANES, LANES), lambda j, te, tv, rt: (j, 0)),
        scratch_shapes=[pltpu.VMEM((tm * SUBLANES, LANES), F32) for _ in range(GATHER_BUFS)]
                       + [pltpu.SemaphoreType.DMA((GATHER_BUFS,)),
                        pltpu.VMEM((d, de), BF16), pltpu.VMEM((d, de), BF16), pltpu.VMEM((de, d), BF16)],
    )
    return pl.pallas_call(
        _expert_kernel,
        grid_spec=grid_spec,
        out_shape=jax.ShapeDtypeStruct((rows * SUBLANES, LANES), F32),
        compiler_params=_params("arbitrary"),
        name="experts",
    )(tile_expert, tile_valid, row_tok, h2_tiles, w1, w3, wd)


def _combine_kernel(pos_ref,
                    ye_hbm, x1_ref, route_ref, g3_ref, gt2_ref,
                    out_ref, gbuf_0, gbuf_1, gbuf_2, sems, *, n_tokens):
    tm, d = x1_ref.shape
    i = pl.program_id(0)
    nt = pl.num_programs(0)
    phase = i % GATHER_BUFS
    bufs = (gbuf_0, gbuf_1, gbuf_2)

    def fetch(tile_index, buf, sem, issue):
        for k in range(TOP_K_IN_GROUP):
            issue(ye_hbm, pos_ref, k * n_tokens + tile_index * tm, buf.at[k], sem, tm)

    @pl.when(i == 0)
    def _():
        for t in range(GATHER_AHEAD):
            fetch(jnp.minimum(t, nt - 1), bufs[t], sems.at[t], _row_gather)

    def tile(p):
        cur = bufs[p]
        for k in range(TOP_K_IN_GROUP):
            _row_gather_wait(ye_hbm, cur.at[k], sems.at[p])
        ahead = (p + GATHER_AHEAD) % GATHER_BUFS
        fetch(jnp.minimum(i + GATHER_AHEAD, nt - 1), bufs[ahead], sems.at[ahead], _row_gather_inline)
        wts = [route_ref[:, TOP_K_IN_GROUP + k:TOP_K_IN_GROUP + k + 1] for k in range(TOP_K_IN_GROUP)]
        y = jnp.concatenate(
            [sum(wts[k] * cur[k, pl.ds(c, tm, stride=SUBLANES), :] for k in range(TOP_K_IN_GROUP))
             for c in range(d // LANES)], axis=1)
        out_ref[...] = x1_ref[...] + gt2_ref[...] * (_rms(y) * g3_ref[...])

        @pl.when(i == nt - 1)
        def _():
            for t in range(1, GATHER_BUFS):
                q = (p + t) % GATHER_BUFS
                for k in range(TOP_K_IN_GROUP):
                    _row_gather_wait(ye_hbm, bufs[q].at[k], sems.at[q])

    for p in range(GATHER_BUFS):
        pl.when(phase == p)(functools.partial(tile, p))


def _combine(ye_tiles, pos, x1, route, g3, gt2, seq):
    n, d = x1.shape
    tm = min(TM_COMB, seq)
    tpb = seq // tm
    kernel = functools.partial(_combine_kernel, n_tokens=n)
    grid_spec = pltpu.PrefetchScalarGridSpec(
        num_scalar_prefetch=1,
        grid=(n // tm,),
        in_specs=[pl.BlockSpec(memory_space=pl.ANY),
                  pl.BlockSpec((tm, d), lambda i, p: (i, 0)),
                  pl.BlockSpec((tm, LANES), lambda i, p: (i, 0)),
                  pl.BlockSpec((1, d), lambda i, p: (0, 0)),
                  pl.BlockSpec((None, 1, d), lambda i, p: (i // tpb, 0, 0))],
        out_specs=pl.BlockSpec((tm, d), lambda i, p: (i, 0)),
        scratch_shapes=[pltpu.VMEM((TOP_K_IN_GROUP, tm * SUBLANES, LANES), F32) for _ in range(GATHER_BUFS)]
                       + [pltpu.SemaphoreType.DMA((GATHER_BUFS,))],
    )
    return pl.pallas_call(
        kernel,
        grid_spec=grid_spec,
        out_shape=jax.ShapeDtypeStruct((n, d), F32),
        compiler_params=_params("arbitrary"),
        name="combine",
    )(pos, ye_tiles, x1, route, g3, gt2)


def _row_token_kernel(pos_ref, out_ref, *, n_tokens):
    u = SCATTER_UNROLL

    def zero(i, _):
        for j in range(u):
            out_ref[i * u + j] = 0
        return 0
    lax.fori_loop(0, out_ref.shape[0] // u, zero, 0)

    for k in range(TOP_K_IN_GROUP):
        def body(i, _):
            for j in range(u):
                t = i * u + j
                out_ref[pos_ref[k * n_tokens + t]] = t
            return 0
        lax.fori_loop(0, n_tokens // u, body, 0)


def _row_tokens(pos, rows, n_tokens):
    assert rows % SCATTER_UNROLL == 0 and n_tokens % SCATTER_UNROLL == 0
    return pl.pallas_call(
        functools.partial(_row_token_kernel, n_tokens=n_tokens),
        in_specs=[pl.BlockSpec(memory_space=pltpu.SMEM)],
        out_specs=pl.BlockSpec(memory_space=pltpu.SMEM),
        out_shape=jax.ShapeDtypeStruct((rows,), jnp.int32),
        name="row_tokens",
    )(pos)


def _plan_rows(route, n_experts, tm):
    n = route.shape[0]
    pairs = TOP_K_IN_GROUP * n
    expert = jnp.concatenate([route[:, k] for k in range(TOP_K_IN_GROUP)]).astype(jnp.int32)
    onehot = (expert[:, None] == jnp.arange(n_experts, dtype=jnp.int32)[None, :]).astype(jnp.int32)
    rank = jnp.sum(jnp.cumsum(onehot, axis=0) * onehot, axis=1) - 1
    counts = jnp.sum(onehot, axis=0)
    padded = -(-counts // tm) * tm
    ends = jnp.cumsum(padded)
    offsets = ends - padded
    pos = jnp.sum(onehot * offsets[None, :], axis=1) + rank
    n_tiles = pairs // tm + n_experts + GATHER_AHEAD
    rows = n_tiles * tm
    row_tok = _row_tokens(pos.astype(jnp.int32), rows, n)
    tile_start = jnp.arange(n_tiles, dtype=jnp.int32) * tm
    tile_expert = jnp.minimum(jnp.sum((ends[None, :] <= tile_start[:, None]).astype(jnp.int32), axis=1),
                              n_experts - 1)
    tile_valid = (tile_start < ends[-1]).astype(jnp.int32)
    return tile_expert, tile_valid, row_tok, pos.astype(jnp.int32)


def _layer(x, c, w_mod, b_mod, norm_gains, w_in, gate_bias, shift_mu, w0, w_decay_up, a0, w_iclr_up,
           w_gate_up, k_k, k_a, r_k, lnx_w, lnx_b, w_o_sb, w_o_rw, w_out, w_group, w_router,
           w_up1, w_up3, w_down):
    batch, seq, d = x.shape
    n = batch * seq
    sb_width = w_o_sb.shape[0]
    rw_width = w_o_rw.shape[0]
    decay_lora, iclr_lora, gate_lora = w_decay_up.shape[0], w_iclr_up.shape[0], w_gate_up.shape[0]
    lora = decay_lora + iclr_lora + gate_lora
    lora_pad = -(-lora // LANES) * LANES
    n_groups = w_group.shape[1]
    n_experts = w_router.shape[1]
    per_group = n_experts // n_groups

    mod = _modulation(c, w_mod, b_mod)
    sh1, sc1, gt1, sh2, sc2, gt2 = [m.reshape(batch, 1, d) for m in jnp.split(mod, 6, axis=-1)]
    gains = norm_gains.reshape(4, 1, d)

    o1 = 3 * sb_width
    o2 = o1 + 3 * rw_width
    o3 = o2 + lora
    wq = w_in[:, :o1].astype(BF16)
    wr = w_in[:, o1:o2].astype(BF16)
    wl = jnp.zeros((d, lora_pad), BF16).at[:, :lora].set(w_in[:, o2:o3].astype(BF16))
    wg = w_in[:, o3:].astype(BF16)
    x2 = x.reshape(n, d)
    qkv, rkv, lora_cols, gate_logits = _inproj(x2, gains[0], sc1, sh1, wq, wr, wl, wg, seq)

    y_sb = _sb_attention(qkv, batch, seq, sb_width)

    mu_rkv = shift_mu[:3 * rw_width]
    mu_lora = jnp.zeros((lora_pad,), F32).at[:lora].set(shift_mu[3 * rw_width:])
    l1 = decay_lora
    l2 = l1 + iclr_lora
    wd_pad = jnp.zeros((lora_pad, rw_width), F32).at[:l1].set(w_decay_up)
    wa_pad = jnp.zeros((lora_pad, rw_width), F32).at[l1:l2].set(w_iclr_up)
    wg_pad = jnp.zeros((lora_pad, rw_width), F32).at[l2:lora].set(w_gate_up)
    head = jnp.arange(rw_width) // HEAD_DIM
    hsum = (head[:, None] == head[None, :]).astype(BF16)
    q_c, yv_c, bonus, g, m_c, n_c = _rw_chunk(rkv, lora_cols, mu_rkv, mu_lora, w0, wd_pad, a0, wa_pad,
                                              wg_pad, k_k, k_a, r_k.reshape(-1), hsum, seq)
    y_rw = _rw_state(q_c, yv_c, bonus, g, m_c, n_c, lnx_w, lnx_b, hsum, batch, seq)

    wrt = jnp.zeros((d, LANES), F32).at[:, :n_groups].set(w_group)
    wrt = wrt.at[:, n_groups:n_groups + n_experts].set(w_router)
    wrt = jnp.concatenate(_split(wrt), axis=1)
    x1, h2_tiles, route = _merge(x2, y_sb, y_rw, gate_logits, gate_bias, w_o_sb.astype(BF16),
                                 w_o_rw.astype(BF16), w_out.astype(BF16), gains[1], gt1, gains[2],
                                 sc2, sh2, wrt, seq, n_groups, per_group)

    tile_expert, tile_valid, row_tok, pos = _plan_rows(route, n_experts, TM_EXP)
    ye = _experts(h2_tiles, tile_expert, tile_valid, row_tok,
                  w_up1, w_up3, w_down)
    out = _combine(ye, pos, x1, route, gains[3], gt2, seq)
    return out.reshape(batch, seq, d)


def kernel(x, c, w_mod, b_mod, norm_gains, w_in, gate_bias, shift_mu, w0, w_decay_up, a0, w_iclr_up,
           w_gate_up, k_k, k_a, r_k, lnx_w, lnx_b, w_o_sb, w_o_rw, w_out, w_group, w_router,
           w_up1, w_up3, w_down):
    for l in range(w_mod.shape[0]):
        x = _layer(x, c, w_mod[l], b_mod[l], norm_gains[l], w_in[l], gate_bias[l], shift_mu[l], w0[l],
                   w_decay_up[l], a0[l], w_iclr_up[l], w_gate_up[l], k_k[l], k_a[l], r_k[l], lnx_w[l],
                   lnx_b[l], w_o_sb[l], w_o_rw[l], w_out[l], w_group[l], w_router[l], w_up1[l],
                   w_up3[l], w_down[l])
    return x
```
